```python
import math
import jax, jax.numpy as jnp
from jax import lax
import numpy as np

D_MODEL = 2048
BATCH = 4
SEQ = 4096
DEPTH = 4
DEC_BATCH = 16
DEC_SEQ = 64
PAST_LEN = 2048

CHUNK = 64
N_MIXERS = 2
N_GLA = (DEPTH + 1) // 2
N_GM = DEPTH // 2
GLA_HEADS = 4
GLA_DK = D_MODEL // 2 // GLA_HEADS
GLA_DV = D_MODEL // GLA_HEADS
GLA_RANK = 16
GLA_TAU = 16.0
GLA_BLOCK = CHUNK
GLA_IN = 2 * GLA_HEADS * GLA_DK + 2 * GLA_HEADS * GLA_DV + GLA_RANK
GM_WIDTH = D_MODEL
GM_GROUPS = 8
GM_GROUP_DIM = GM_WIDTH // GM_GROUPS
GM_CHUNK = 128
D_FF = 5632
CONV_W = 3
PE_DIM = 256
EPS = 1e-6

kernel_name = "gla_gmlp_convffn_streaming_step"


def rms_norm(x, g):
    xf = x.astype(jnp.float32)
    y = xf * lax.rsqrt(jnp.mean(xf * xf, axis=-1, keepdims=True) + EPS)
    return (y * g.astype(jnp.float32)).astype(x.dtype)


def layer_norm(x, g, b):
    xf = x.astype(jnp.float32)
    mu = jnp.mean(xf, axis=-1, keepdims=True)
    var = jnp.mean(jnp.square(xf - mu), axis=-1, keepdims=True)
    y = (xf - mu) * lax.rsqrt(var + EPS)
    return (y * g.astype(jnp.float32) + b.astype(jnp.float32)).astype(x.dtype)


def gla_recurrence(q, k, v, log_a, s0):
    B, T, H, DK = q.shape
    DV = v.shape[-1]
    blk = T if T <= GLA_BLOCK else math.gcd(T, GLA_BLOCK)
    n = T // blk

    def blocks(a):
        return a.astype(jnp.float32).reshape(B, n, blk, H, a.shape[-1]).transpose(1, 0, 3, 2, 4)

    qb, kb, vb, gb = blocks(q), blocks(k), blocks(v), blocks(log_a)
    cum = jnp.cumsum(gb, axis=3)
    last = cum[:, :, :, -1:, :]
    q_dec = qb * jnp.exp(cum)
    k_inv = kb * jnp.exp(-cum)
    k_end = kb * jnp.exp(last - cum)
    causal = jnp.tril(jnp.ones((blk, blk), dtype=bool))
    att = jnp.where(causal, jnp.einsum('nbhik,nbhjk->nbhij', q_dec, k_inv), 0.0)
    o_intra = jnp.einsum('nbhij,nbhjv->nbhiv', att, vb)

    def step(s, xs):
        qd, ke, vv, la = xs
        o = jnp.einsum('bhik,bhkv->bhiv', qd, s)
        s = s * jnp.exp(la[:, :, 0, :, None]) + jnp.einsum('bhjk,bhjv->bhkv', ke, vv)
        return s, o

    s_fin, o_inter = lax.scan(step, s0.astype(jnp.float32), (q_dec, k_end, vb, last))
    o = (o_intra + o_inter).transpose(1, 0, 3, 2, 4).reshape(B, T, H, DV)
    return o, s_fin


def gla_mixer(h, w_in, w_gate, b_gate, out_g, w_o, s0):
    B, T, _ = h.shape
    hk = GLA_HEADS * GLA_DK
    hv = GLA_HEADS * GLA_DV
    proj = h @ w_in
    q, k, v, r, gl = jnp.split(proj, [hk, 2 * hk, 2 * hk + hv, 2 * hk + 2 * hv], axis=-1)
    log_a = jax.nn.log_sigmoid((gl @ w_gate + b_gate).astype(jnp.float32)) / GLA_TAU
    q = q.reshape(B, T, GLA_HEADS, GLA_DK) * (GLA_DK ** -0.5)
    k = k.reshape(B, T, GLA_HEADS, GLA_DK)
    v = v.reshape(B, T, GLA_HEADS, GLA_DV)
    log_a = log_a.reshape(B, T, GLA_HEADS, GLA_DK)
    o, s = gla_recurrence(q, k, v, log_a, s0)
    o = rms_norm(o, out_g).reshape(B, T, hv).astype(h.dtype) * jax.nn.silu(r)
    return o @ w_o, s


def gmlp_mixer(h, w_in, b_in, ln_g, ln_b, w_s, b_s, w_o):
    B, T, _ = h.shape
    z = jax.nn.gelu(h @ w_in + b_in)
    u, v = jnp.split(z, 2, axis=-1)
    v = layer_norm(v, ln_g, ln_b)
    L = min(T, GM_CHUNK)
    n = T // L
    ws = w_s[:, :L, :L] * jnp.tril(jnp.ones((L, L), w_s.dtype))
    vb = v.reshape(B, n, L, GM_GROUPS, GM_GROUP_DIM)
    mixed = jnp.einsum('gij,bnjgc->bnigc', ws, vb) + b_s[:, :L].T[None, None, :, :, None]
    out = u * mixed.reshape(B, T, GM_WIDTH)
    return out @ w_o, v


def conv_ffn(h, w_up, conv_w, conv_b, w_down, past):
    T = h.shape[1]
    a = h @ w_up
    xp = jnp.concatenate([past.astype(a.dtype), a], axis=1)
    c = conv_b + xp[:, 0:T] * conv_w[0]
    for j in range(1, CONV_W):
        c = c + xp[:, j:j + T] * conv_w[j]
    g, val = jnp.split(c, 2, axis=-1)
    return (jax.nn.silu(g) * val) @ w_down, xp[:, -(CONV_W - 1):]


def per_layer_embed(h, p_i, g, w_proj, w_gate):
    gate = jax.nn.sigmoid(rms_norm(h, g) @ w_gate)
    return gate * (p_i @ w_proj)


def _trunk(x, p, gla_s0, conv_s0, prm):
    h = x
    gla_states, conv_states, gm_rows = [], [], []
    for i in range(DEPTH):
        hn = rms_norm(h, prm['norm_mix'][i])
        j = i // N_MIXERS
        if i % N_MIXERS == 0:
            mix, s = gla_mixer(hn, prm['gla_w_in'][j], prm['gla_w_gate'][j], prm['gla_b_gate'][j],
                               prm['gla_out_norm'][j], prm['gla_w_o'][j], gla_s0[j])
            gla_states.append(s)
        else:
            mix, vrows = gmlp_mixer(hn, prm['gm_w_in'][j], prm['gm_b_in'][j], prm['gm_ln_g'][j],
                                    prm['gm_ln_b'][j], prm['gm_w_s'][j], prm['gm_b_s'][j], prm['gm_w_o'][j])
            gm_rows.append(vrows)
        h = h + mix
        f, cs = conv_ffn(rms_norm(h, prm['norm_ffn'][i]), prm['ffn_w_up'][i], prm['ffn_conv_w'][i],
                         prm['ffn_conv_b'][i], prm['ffn_w_down'][i], conv_s0[i])
        conv_states.append(cs)
        h = h + f
        h = h + per_layer_embed(h, p[i], prm['norm_pe'][i], prm['pe_w_proj'][i], prm['pe_w_gate'][i])
    y = rms_norm(h, prm['norm_final'])
    return y, jnp.stack(gla_states), jnp.stack(conv_states), jnp.stack(gm_rows)


def setup_inputs(seed: int = 0) -> dict:
    key = jax.random.key(seed)
    ks = iter(jax.random.split(key, 32))

    def nrm(shape, scale):
        return jax.random.normal(next(ks), shape, jnp.float32) * scale

    def gain(shape):
        return 1.0 + nrm(shape, 0.02)

    hk = GLA_HEADS * GLA_DK
    hv = GLA_HEADS * GLA_DV
    return {
        'x_prompt': nrm((BATCH, SEQ, D_MODEL), 1.0),
        'x_sample': nrm((DEC_BATCH, DEC_SEQ, D_MODEL), 1.0),
        'p_prompt': nrm((DEPTH, BATCH, SEQ, PE_DIM), 1.0),
        'p_sample': nrm((DEPTH, DEC_BATCH, DEC_SEQ, PE_DIM), 1.0),
        'state_gla': nrm((N_GLA, DEC_BATCH, GLA_HEADS, GLA_DK, GLA_DV), 0.5),
        'state_ffn_conv': nrm((DEPTH, DEC_BATCH, CONV_W - 1, 2 * D_FF), 1.0),
        'norm_mix': gain((DEPTH, D_MODEL)),
        'norm_ffn': gain((DEPTH, D_MODEL)),
        'norm_pe': gain((DEPTH, D_MODEL)),
        'norm_final': gain((D_MODEL,)),
        'gla_w_in': nrm((N_GLA, D_MODEL, GLA_IN), D_MODEL ** -0.5),
        'gla_w_gate': nrm((N_GLA, GLA_RANK, hk), GLA_RANK ** -0.5),
        'gla_b_gate': nrm((N_GLA, hk), 0.1),
        'gla_out_norm': gain((N_GLA, GLA_DV)),
        'gla_w_o': nrm((N_GLA, hv, D_MODEL), hv ** -0.5),
        'gm_w_in': nrm((N_GM, D_MODEL, 2 * GM_WIDTH), D_MODEL ** -0.5),
        'gm_b_in': nrm((N_GM, 2 * GM_WIDTH), 0.02),
        'gm_ln_g': gain((N_GM, GM_WIDTH)),
        'gm_ln_b': nrm((N_GM, GM_WIDTH), 0.02),
        'gm_w_s': nrm((N_GM, GM_GROUPS, GM_CHUNK, GM_CHUNK), 0.05),
        'gm_b_s': gain((N_GM, GM_GROUPS, GM_CHUNK)),
        'gm_w_o': nrm((N_GM, GM_WIDTH, D_MODEL), GM_WIDTH ** -0.5),
        'ffn_w_up': nrm((DEPTH, D_MODEL, 2 * D_FF), D_MODEL ** -0.5),
        'ffn_conv_w': nrm((DEPTH, CONV_W, 2 * D_FF), CONV_W ** -0.5),
        'ffn_conv_b': nrm((DEPTH, 2 * D_FF), 0.02),
        'ffn_w_down': nrm((DEPTH, D_FF, D_MODEL), D_FF ** -0.5),
        'pe_w_proj': nrm((DEPTH, PE_DIM, D_MODEL), PE_DIM ** -0.5),
        'pe_w_gate': nrm((DEPTH, D_MODEL, D_MODEL), D_MODEL ** -0.5),
    }


def reference(x_prompt, x_sample, p_prompt, p_sample, state_gla, state_ffn_conv,
              norm_mix, norm_ffn, norm_pe, norm_final,
              gla_w_in, gla_w_gate, gla_b_gate, gla_out_norm, gla_w_o,
              gm_w_in, gm_b_in, gm_ln_g, gm_ln_b, gm_w_s, gm_b_s, gm_w_o,
              ffn_w_up, ffn_conv_w, ffn_conv_b, ffn_w_down,
              pe_w_proj, pe_w_gate):
    prm = {
        'norm_mix': norm_mix, 'norm_ffn': norm_ffn, 'norm_pe': norm_pe, 'norm_final': norm_final,
        'gla_w_in': gla_w_in, 'gla_w_gate': gla_w_gate, 'gla_b_gate': gla_b_gate,
        'gla_out_norm': gla_out_norm, 'gla_w_o': gla_w_o,
        'gm_w_in': gm_w_in, 'gm_b_in': gm_b_in, 'gm_ln_g': gm_ln_g, 'gm_ln_b': gm_ln_b,
        'gm_w_s': gm_w_s, 'gm_b_s': gm_b_s, 'gm_w_o': gm_w_o,
        'ffn_w_up': ffn_w_up, 'ffn_conv_w': ffn_conv_w, 'ffn_conv_b': ffn_conv_b, 'ffn_w_down': ffn_w_down,
        'pe_w_proj': pe_w_proj, 'pe_w_gate': pe_w_gate,
    }
    b_prompt = x_prompt.shape[0]
    gla_s0_prompt = jnp.zeros((N_GLA, b_prompt, GLA_HEADS, GLA_DK, GLA_DV), jnp.float32)
    conv_s0_prompt = jnp.zeros((DEPTH, b_prompt, CONV_W - 1, 2 * D_FF), x_prompt.dtype)
    y_prompt, gla_state_prompt, conv_state_prompt, _ = _trunk(
        x_prompt, p_prompt, gla_s0_prompt, conv_s0_prompt, prm)
    y_sample, gla_state_sample, conv_state_sample, gm_v_sample = _trunk(
        x_sample, p_sample, state_gla, state_ffn_conv, prm)
    return (y_prompt, y_sample, gla_state_prompt, gla_state_sample,
            conv_state_prompt, conv_state_sample, gm_v_sample)
```

```python
import functools

import jax
import jax.numpy as jnp
from jax import lax
from jax.experimental import pallas as pl
from jax.experimental.pallas import tpu as pltpu

F32 = jnp.float32
BF16 = jnp.bfloat16

D_MODEL = 2048
DEPTH = 4
GLA_HEADS = 4
GLA_DK = 256
GLA_DV = 512
GLA_HK = GLA_HEADS * GLA_DK
GLA_HV = GLA_HEADS * GLA_DV
GLA_RANK = 16
GLA_TAU = 16.0
GLA_BLOCK = 64
GM_WIDTH = 2048
GM_GROUPS = 8
GM_GROUP_DIM = GM_WIDTH // GM_GROUPS
GM_CHUNK = 128
D_FF = 5632
CONV_W = 3
PE_DIM = 256
EPS = 1e-6

V7X_VMEM_BYTES = 64 * 1024 * 1024
V7X_LANES = 128
V7X_SUBLANES = 8

ARB2 = ("arbitrary", "arbitrary")
ARB1 = ("arbitrary",)


def _params(sem, vmem_bytes):
    assert vmem_bytes <= V7X_VMEM_BYTES - (2 << 20), vmem_bytes
    return pltpu.CompilerParams(dimension_semantics=sem, vmem_limit_bytes=int(vmem_bytes))


def _nbytes(shape, dtype):
    n = 1
    for s in shape:
        n *= s
    return n * jnp.dtype(dtype).itemsize


def _rms(x, g):
    return x * lax.rsqrt(jnp.mean(x * x, axis=-1, keepdims=True) + EPS) * g


def _log_sigmoid(x):
    return jnp.minimum(x, 0.0) - jnp.log1p(jnp.exp(-jnp.abs(x)))


def _dot(a, b):
    return jnp.dot(a, b, preferred_element_type=F32)


def _gla_in_kernel(h_ref, g_ref, w_ref, wgd_ref, wgate_ref, bgate_ref,
                   proj_ref, loga_ref, xn_ref):
    @pl.when(pl.program_id(1) == 0)
    def _():
        xn = _rms(h_ref[...], g_ref[...]).astype(BF16)
        xn_ref[...] = xn
        gl = _dot(xn, wgd_ref[...])
        z = _dot(gl.astype(BF16), wgate_ref[...]) + bgate_ref[...]
        loga_ref[...] = _log_sigmoid(z) * (1.0 / GLA_TAU)

    proj_ref[...] = _dot(xn_ref[...], w_ref[...])


def _gla_in(h, g, w_main, w_gd, w_gate, b_gate, layer_g, layer):
    m = h.shape[0]
    tm, tn = 1024, 1024
    n_main = w_main.shape[-1]
    vmem = (2 * _nbytes((tm, D_MODEL), F32) + _nbytes((tm, D_MODEL), BF16)
            + 2 * _nbytes((D_MODEL, tn), BF16) + 2 * _nbytes((tm, tn), F32)
            + 2 * _nbytes((tm, GLA_HK), F32) + 3 * _nbytes((tm, GLA_HK), F32)
            + 4 * _nbytes((D_MODEL, V7X_LANES), BF16))
    return pl.pallas_call(
        _gla_in_kernel,
        grid=(m // tm, n_main // tn),
        in_specs=[
            pl.BlockSpec((tm, D_MODEL), lambda i, j: (i, 0)),
            pl.BlockSpec((None, 1, D_MODEL), lambda i, j: (layer_g, 0, 0)),
            pl.BlockSpec((None, D_MODEL, tn), lambda i, j: (layer, 0, j)),
            pl.BlockSpec((None, D_MODEL, V7X_LANES), lambda i, j: (layer, 0, 0)),
            pl.BlockSpec((None, V7X_LANES, GLA_HK), lambda i, j: (layer, 0, 0)),
            pl.BlockSpec((None, 1, GLA_HK), lambda i, j: (layer, 0, 0)),
        ],
        out_specs=[
            pl.BlockSpec((tm, tn), lambda i, j: (i, j)),
            pl.BlockSpec((tm, GLA_HK), lambda i, j: (i, 0)),
        ],
        out_shape=[jax.ShapeDtypeStruct((m, n_main), F32),
                   jax.ShapeDtypeStruct((m, GLA_HK), F32)],
        scratch_shapes=[pltpu.VMEM((tm, D_MODEL), BF16)],
        compiler_params=_params(ARB2, vmem),
        name="gla_in",
    )(h, g, w_main, w_gd, w_gate, b_gate)


def _gla_rec_kernel(q_ref, k_ref, v_ref, r_ref, la_ref, s0_ref, gain_ref,
                    o_ref, s_ref, *, n_blocks):
    blk = GLA_BLOCK

    @pl.when(pl.program_id(1) == 0)
    def _():
        s_ref[...] = s0_ref[...]

    row = lax.broadcasted_iota(jnp.int32, (blk, blk), 0)
    col = lax.broadcasted_iota(jnp.int32, (blk, blk), 1)
    causal = row >= col
    tri = causal.astype(F32).astype(BF16)
    out_g = gain_ref[...]

    def block(n, carry):
        rows = pl.ds(pl.multiple_of(n * blk, blk), blk)
        g = la_ref[rows, :]
        g1 = g.astype(BF16)
        e1 = g - g1.astype(F32)
        g2 = e1.astype(BF16)
        g3 = (e1 - g2.astype(F32)).astype(BF16)
        cum = _dot(tri, g1) + _dot(tri, g2) + _dot(tri, g3)
        last = cum[blk - 1:blk, :]
        q = q_ref[rows, :] * (GLA_DK ** -0.5)
        k = k_ref[rows, :]
        q_dec = (q * jnp.exp(cum)).astype(BF16)
        k_inv = (k * jnp.exp(-cum)).astype(BF16)
        k_end = (k * jnp.exp(last - cum)).astype(BF16)
        decay = jnp.exp(last)
        for hh in range(GLA_HEADS):
            ks = slice(hh * GLA_DK, (hh + 1) * GLA_DK)
            vs = slice(hh * GLA_DV, (hh + 1) * GLA_DV)
            att = lax.dot_general(q_dec[:, ks], k_inv[:, ks], (((1,), (1,)), ((), ())),
                                  preferred_element_type=F32)
            att = jnp.where(causal, att, 0.0)
            vb = v_ref[rows, vs].astype(BF16)
            s = s_ref[0, hh]
            o = _dot(att.astype(BF16), vb) + _dot(q_dec[:, ks], s.astype(BF16))
            kv = lax.dot_general(k_end[:, ks], vb, (((0,), (0,)), ((), ())),
                                 preferred_element_type=F32)
            dcol = jnp.broadcast_to(decay[:, ks], (V7X_LANES, GLA_DK)).T
            s_ref[0, hh] = s * jnp.tile(dcol, (1, GLA_DV // V7X_LANES)) + kv
            on = _rms(o, out_g)
            r = r_ref[rows, vs]
            o_ref[rows, vs] = (on * (r * jax.nn.sigmoid(r))).astype(BF16)
        return carry

    lax.fori_loop(0, n_blocks, block, 0)


def _gla_rec(proj, loga, s0, out_g, layer, n_seq, seq_len):
    m = proj.shape[0]
    tc = min(seq_len, 256)
    nt = seq_len // tc
    row_map = lambda c: (lambda b, t: (b * nt + t, c))
    st_spec = pl.BlockSpec((1, GLA_HEADS, GLA_DK, GLA_DV), lambda b, t: (b, 0, 0, 0))
    s0_spec = pl.BlockSpec((None, 1, GLA_HEADS, GLA_DK, GLA_DV), lambda b, t: (layer, b, 0, 0, 0))
    vmem = (2 * (3 * _nbytes((tc, GLA_HK), F32) + 2 * _nbytes((tc, GLA_HV), F32)
                 + _nbytes((tc, GLA_HV), BF16)
                 + 2 * _nbytes((GLA_HEADS, GLA_DK, GLA_DV), F32))
            + 16 * _nbytes((GLA_BLOCK, GLA_HK), F32) + 6 * _nbytes((GLA_DK, GLA_DV), F32))
    return pl.pallas_call(
        functools.partial(_gla_rec_kernel, n_blocks=tc // GLA_BLOCK),
        grid=(n_seq, nt),
        in_specs=[
            pl.BlockSpec((tc, GLA_HK), row_map(0)),
            pl.BlockSpec((tc, GLA_HK), row_map(1)),
            pl.BlockSpec((tc, GLA_HV), row_map(1)),
            pl.BlockSpec((tc, GLA_HV), row_map(2)),
            pl.BlockSpec((tc, GLA_HK), row_map(0)),
            s0_spec,
            pl.BlockSpec((None, 1, GLA_DV), lambda b, t: (layer, 0, 0)),
        ],
        out_specs=[pl.BlockSpec((tc, GLA_HV), row_map(0)), st_spec],
        out_shape=[jax.ShapeDtypeStruct((m, GLA_HV), BF16),
                   jax.ShapeDtypeStruct((n_seq, GLA_HEADS, GLA_DK, GLA_DV), F32)],
        compiler_params=_params(ARB2, vmem),
        name="gla_rec",
    )(proj, proj, proj, proj, loga, s0, out_g)


def _out_proj_kernel(x_ref, w_ref, res_ref, g_ref, h_ref, xn_ref):
    h = res_ref[...] + _dot(x_ref[...], w_ref[...])
    h_ref[...] = h
    xn_ref[...] = _rms(h, g_ref[...]).astype(BF16)


def _out_proj(x, w, res, g_ffn, layer_w, layer_g):
    m = x.shape[0]
    tm = 512
    vmem = (2 * (2 * _nbytes((tm, D_MODEL), BF16) + 2 * _nbytes((tm, D_MODEL), F32)
                 + _nbytes((D_MODEL, D_MODEL), BF16))
            + 3 * _nbytes((tm, D_MODEL), F32))
    row = pl.BlockSpec((tm, D_MODEL), lambda i: (i, 0))
    return pl.pallas_call(
        _out_proj_kernel,
        grid=(m // tm,),
        in_specs=[
            row,
            pl.BlockSpec((None, D_MODEL, D_MODEL), lambda i: (layer_w, 0, 0)),
            row,
            pl.BlockSpec((None, 1, D_MODEL), lambda i: (layer_g, 0, 0)),
        ],
        out_specs=[row, row],
        out_shape=[jax.ShapeDtypeStruct((m, D_MODEL), F32),
                   jax.ShapeDtypeStruct((m, D_MODEL), BF16)],
        compiler_params=_params(ARB1, vmem),
        name="out_proj",
    )(x, w, res, g_ffn)


def _gm_in_kernel(h_ref, g_ref, w_ref, b_ref, z_ref, xn_ref):
    @pl.when(pl.program_id(1) == 0)
    def _():
        xn_ref[...] = _rms(h_ref[...], g_ref[...]).astype(BF16)

    z_ref[...] = jax.nn.gelu(_dot(xn_ref[...], w_ref[...]) + b_ref[...])


def _gm_in(h, g, w, b, layer_g, layer_w):
    m = h.shape[0]
    tm, tn = 1024, 1024
    n = w.shape[-1]
    vmem = (2 * _nbytes((tm, D_MODEL), F32) + _nbytes((tm, D_MODEL), BF16)
            + 2 * _nbytes((D_MODEL, tn), BF16) + 2 * _nbytes((tm, tn), F32)
            + 3 * _nbytes((tm, tn), F32))
    return pl.pallas_call(
        _gm_in_kernel,
        grid=(m // tm, n // tn),
        in_specs=[
            pl.BlockSpec((tm, D_MODEL), lambda i, j: (i, 0)),
            pl.BlockSpec((None, 1, D_MODEL), lambda i, j: (layer_g, 0, 0)),
            pl.BlockSpec((None, D_MODEL, tn), lambda i, j: (layer_w, 0, j)),
            pl.BlockSpec((None, 1, tn), lambda i, j: (layer_w, 0, j)),
        ],
        out_specs=pl.BlockSpec((tm, tn), lambda i, j: (i, j)),
        out_shape=jax.ShapeDtypeStruct((m, n), F32),
        scratch_shapes=[pltpu.VMEM((tm, D_MODEL), BF16)],
        compiler_params=_params(ARB2, vmem),
        name="gm_in",
    )(h, g, w, b)


def _gm_out_kernel(z_ref, lng_ref, lnb_ref, ws_ref, bs_ref, wo_ref, res_ref, g_ref,
                   h_ref, xn_ref, *rest, chunk, n_chunks, emit_v):
    if emit_v:
        vn_ref, gate_ref = rest
    else:
        (gate_ref,) = rest
    v = z_ref[:, GM_WIDTH:]
    mu = jnp.mean(v, axis=-1, keepdims=True)
    vc = v - mu
    var = jnp.mean(vc * vc, axis=-1, keepdims=True)
    vn = vc * lax.rsqrt(var + EPS) * lng_ref[...] + lnb_ref[...]
    if emit_v:
        vn_ref[...] = vn
    vb = vn.astype(BF16)

    row = lax.broadcasted_iota(jnp.int32, (chunk, chunk), 0)
    col = lax.broadcasted_iota(jnp.int32, (chunk, chunk), 1)
    tril = (row >= col).astype(F32)
    for gi in range(GM_GROUPS):
        cs = slice(gi * GM_GROUP_DIM, (gi + 1) * GM_GROUP_DIM)
        wsg = (ws_ref[gi] * tril).astype(BF16)
        bsg = bs_ref[:, gi:gi + 1]
        for c in range(n_chunks):
            rows = slice(c * chunk, (c + 1) * chunk)
            mixed = _dot(wsg, vb[rows, cs]) + bsg
            gate_ref[rows, cs] = (z_ref[rows, cs] * mixed).astype(BF16)

    h = res_ref[...] + _dot(gate_ref[...], wo_ref[...])
    h_ref[...] = h
    xn_ref[...] = _rms(h, g_ref[...]).astype(BF16)


def _gm_out(z, ln_g, ln_b, ws, bs_t, wo, res, g_ffn, layer_m, layer_g, chunk, emit_v):
    m = z.shape[0]
    tm = 256
    row = pl.BlockSpec((tm, D_MODEL), lambda i: (i, 0))
    vec = pl.BlockSpec((None, 1, D_MODEL), lambda i: (layer_m, 0, 0))
    vmem = (2 * (_nbytes((tm, 2 * GM_WIDTH), F32) + 3 * _nbytes((tm, D_MODEL), F32)
                 + _nbytes((tm, D_MODEL), BF16) + _nbytes((D_MODEL, D_MODEL), BF16)
                 + _nbytes((GM_GROUPS, chunk, chunk), F32))
            + 2 * _nbytes((tm, D_MODEL), BF16) + 4 * _nbytes((tm, D_MODEL), F32))
    out_specs = [row, row]
    out_shape = [jax.ShapeDtypeStruct((m, D_MODEL), F32), jax.ShapeDtypeStruct((m, D_MODEL), BF16)]
    if emit_v:
        out_specs.append(row)
        out_shape.append(jax.ShapeDtypeStruct((m, GM_WIDTH), F32))
    return pl.pallas_call(
        functools.partial(_gm_out_kernel, chunk=chunk, n_chunks=tm // chunk, emit_v=emit_v),
        grid=(m // tm,),
        in_specs=[
            pl.BlockSpec((tm, 2 * GM_WIDTH), lambda i: (i, 0)),
            vec,
            vec,
            pl.BlockSpec((GM_GROUPS, chunk, chunk), lambda i: (0, 0, 0)),
            pl.BlockSpec((chunk, GM_GROUPS), lambda i: (0, 0)),
            pl.BlockSpec((None, D_MODEL, D_MODEL), lambda i: (layer_m, 0, 0)),
            row,
            pl.BlockSpec((None, 1, D_MODEL), lambda i: (layer_g, 0, 0)),
        ],
        out_specs=out_specs,
        out_shape=out_shape,
        scratch_shapes=[pltpu.VMEM((tm, GM_WIDTH), BF16)],
        compiler_params=_params(ARB1, vmem),
        name="gm_out",
    )(z, ln_g, ln_b, ws, bs_t, wo, res, g_ffn)


PAST_ROWS = V7X_SUBLANES


def _ffn_kernel(x_ref, wg_ref, wv_ref, cwg_ref, cwv_ref, cbg_ref, cbv_ref, wd_ref,
                pg_ref, pv_ref, f_ref, csg_ref, csv_ref,
                workg_ref, workv_ref, act_ref, carg_ref, carv_ref,
                *, n_seq, seq_rows, tiles_per_seq):
    i = pl.program_id(0)
    j = pl.program_id(1)
    stride = seq_rows + PAST_ROWS
    x = x_ref[...]

    @pl.when(j == 0)
    def _():
        f_ref[...] = jnp.zeros_like(f_ref)

    if tiles_per_seq > 1:
        @pl.when((i == 0) & (j == 0))
        def _():
            carg_ref[...] = jnp.zeros_like(carg_ref)
            carv_ref[...] = jnp.zeros_like(carv_ref)

    def up(w_ref, past_ref, car_ref, cs_ref, work_ref):
        a = _dot(x, w_ref[...])
        for s in range(n_seq):
            base = s * stride
            work_ref[base + PAST_ROWS:base + stride, :] = a[s * seq_rows:(s + 1) * seq_rows, :]
            cs_ref[s] = a[(s + 1) * seq_rows - PAST_ROWS:(s + 1) * seq_rows, :]
        if tiles_per_seq > 1:
            first = (i % tiles_per_seq) == 0
            work_ref[0:PAST_ROWS, :] = jnp.where(first, past_ref[0], car_ref[j])
            car_ref[j] = a[seq_rows - PAST_ROWS:seq_rows, :]
        else:
            for s in range(n_seq):
                work_ref[s * stride:s * stride + PAST_ROWS, :] = past_ref[s]

    up(wg_ref, pg_ref, carg_ref, csg_ref, workg_ref)
    up(wv_ref, pv_ref, carv_ref, csv_ref, workv_ref)

    def conv(work_ref, cw_ref, cb_ref, s):
        base = s * stride + PAST_ROWS
        x0 = work_ref[base:base + seq_rows, :]
        x1 = work_ref[base - 1:base - 1 + seq_rows, :]
        x2 = work_ref[base - 2:base - 2 + seq_rows, :]
        return cb_ref[...] + x2 * cw_ref[0:1, :] + x1 * cw_ref[1:2, :] + x0 * cw_ref[2:3, :]

    for s in range(n_seq):
        cg = conv(workg_ref, cwg_ref, cbg_ref, s)
        cv = conv(workv_ref, cwv_ref, cbv_ref, s)
        act_ref[s * seq_rows:(s + 1) * seq_rows, :] = (cg * jax.nn.sigmoid(cg) * cv).astype(BF16)

    f_ref[...] += _dot(act_ref[...], wd_ref[...])


def _ffn(xn, w_up, conv_w, conv_b, w_down, past, layer, seq_len):
    m = xn.shape[0]
    tm, tf = 1024, 512
    nj = D_FF // tf
    if seq_len >= tm:
        n_seq, seq_rows, tiles_per_seq = 1, tm, seq_len // tm
    else:
        n_seq, seq_rows, tiles_per_seq = tm // seq_len, seq_len, 1
    n_tiles = m // tm
    past_map = lambda off: (lambda i, j: (i // tiles_per_seq, 0, j + off))
    cs_spec = pl.BlockSpec((n_seq, PAST_ROWS, tf), lambda i, j: (i, 0, j))
    cs_shape = jax.ShapeDtypeStruct((n_tiles * n_seq, PAST_ROWS, D_FF), F32)
    work = pltpu.VMEM((n_seq * (seq_rows + PAST_ROWS), tf), F32)
    car = pltpu.VMEM((nj, PAST_ROWS, tf), F32)
    vmem = (2 * (_nbytes((tm, D_MODEL), BF16) + 3 * _nbytes((D_MODEL, tf), BF16)
                 + _nbytes((tm, D_MODEL), F32) + 4 * _nbytes((n_seq, PAST_ROWS, tf), F32))
            + 2 * _nbytes((tm + n_seq * PAST_ROWS, tf), F32) + _nbytes((tm, tf), BF16)
            + 2 * _nbytes((nj, PAST_ROWS, tf), F32)
            + 4 * _nbytes((tm, tf), F32) + _nbytes((tm, D_MODEL), F32))
    f, csg, csv = pl.pallas_call(
        functools.partial(_ffn_kernel, n_seq=n_seq, seq_rows=seq_rows, tiles_per_seq=tiles_per_seq),
        grid=(n_tiles, nj),
        in_specs=[
            pl.BlockSpec((tm, D_MODEL), lambda i, j: (i, 0)),
            pl.BlockSpec((None, D_MODEL, tf), lambda i, j: (layer, 0, j)),
            pl.BlockSpec((None, D_MODEL, tf), lambda i, j: (layer, 0, j + nj)),
            pl.BlockSpec((None, CONV_W, tf), lambda i, j: (layer, 0, j)),
            pl.BlockSpec((None, CONV_W, tf), lambda i, j: (layer, 0, j + nj)),
            pl.BlockSpec((None, 1, tf), lambda i, j: (layer, 0, j)),
            pl.BlockSpec((None, 1, tf), lambda i, j: (layer, 0, j + nj)),
            pl.BlockSpec((None, tf, D_MODEL), lambda i, j: (layer, j, 0)),
            pl.BlockSpec((n_seq, PAST_ROWS, tf), past_map(0)),
            pl.BlockSpec((n_seq, PAST_ROWS, tf), past_map(nj)),
        ],
        out_specs=[pl.BlockSpec((tm, D_MODEL), lambda i, j: (i, 0)), cs_spec, cs_spec],
        out_shape=[jax.ShapeDtypeStruct((m, D_MODEL), F32), cs_shape, cs_shape],
        scratch_shapes=[work, work, pltpu.VMEM((tm, tf), BF16), car, car],
        compiler_params=_params(ARB2, vmem),
        name="ffn",
    )(xn, w_up, w_up, conv_w, conv_w, conv_b, conv_b, w_down, past, past)
    keep = slice(tiles_per_seq - 1, None, tiles_per_seq)
    cstate = jnp.concatenate([csg[keep], csv[keep]], axis=-1)[:, PAST_ROWS - (CONV_W - 1):, :]
    return f, cstate


def _pe_kernel(h1_ref, f_ref, p_ref, g_ref, wg_ref, wp_ref, gfin_ref, o_ref, *, final):
    h2 = h1_ref[...] + f_ref[...]
    hn = _rms(h2, g_ref[...]).astype(BF16)
    gate = jax.nn.sigmoid(_dot(hn, wg_ref[...]))
    proj = _dot(p_ref[...].astype(BF16), wp_ref[...])
    h3 = h2 + gate * proj
    o_ref[...] = _rms(h3, gfin_ref[...]) if final else h3


def _pe(h1, f, p, g_pe, w_gate, w_proj, g_final, layer):
    m = h1.shape[0]
    tm = 256
    row = pl.BlockSpec((tm, D_MODEL), lambda i: (i, 0))
    vmem = (2 * (3 * _nbytes((tm, D_MODEL), F32) + _nbytes((tm, PE_DIM), F32)
                 + _nbytes((D_MODEL, D_MODEL), BF16) + _nbytes((PE_DIM, D_MODEL), BF16))
            + 6 * _nbytes((tm, D_MODEL), F32))
    return pl.pallas_call(
        functools.partial(_pe_kernel, final=(layer == DEPTH - 1)),
        grid=(m // tm,),
        in_specs=[
            row,
            row,
            pl.BlockSpec((None, tm, PE_DIM), lambda i: (layer, i, 0)),
            pl.BlockSpec((None, 1, D_MODEL), lambda i: (layer, 0, 0)),
            pl.BlockSpec((None, D_MODEL, D_MODEL), lambda i: (layer, 0, 0)),
            pl.BlockSpec((None, PE_DIM, D_MODEL), lambda i: (layer, 0, 0)),
            pl.BlockSpec((1, D_MODEL), lambda i: (0, 0)),
        ],
        out_specs=row,
        out_shape=jax.ShapeDtypeStruct((m, D_MODEL), F32),
        compiler_params=_params(ARB1, vmem),
        name="pe",
    )(h1, f, p, g_pe, w_gate, w_proj, g_final)


def _trunk(x, p, gla_s0, conv_past, w, n_seq, seq_len, emit_v):
    m = n_seq * seq_len
    h = x.reshape(m, D_MODEL)
    p = p.reshape(DEPTH, m, PE_DIM)
    chunk = min(seq_len, GM_CHUNK)
    gla_states, conv_states, gm_rows = [], [], []
    for i in range(DEPTH):
        jm = i // 2
        if i % 2 == 0:
            proj, loga = _gla_in(h, w["norm_mix"], w["gla_w_main"], w["gla_w_gd"],
                                 w["gla_w_gate"], w["gla_b_gate"], i, jm)
            og, s_fin = _gla_rec(proj, loga, gla_s0, w["gla_out_norm"], jm, n_seq, seq_len)
            gla_states.append(s_fin)
            h1, xn = _out_proj(og, w["gla_w_o"], h, w["norm_ffn"], jm, i)
        else:
            z = _gm_in(h, w["norm_mix"], w["gm_w_in"], w["gm_b_in"], i, jm)
            ws = w["gm_w_s"][jm][:, :chunk, :chunk]
            bs_t = w["gm_b_s"][jm][:, :chunk].T
            outs = _gm_out(z, w["gm_ln_g"], w["gm_ln_b"], ws, bs_t, w["gm_w_o"], h,
                           w["norm_ffn"], jm, i, chunk, emit_v)
            h1, xn = outs[0], outs[1]
            if emit_v:
                gm_rows.append(outs[2].reshape(n_seq, seq_len, GM_WIDTH))
        f, cstate = _ffn(xn, w["ffn_w_up"], w["ffn_conv_w"], w["ffn_conv_b"], w["ffn_w_down"],
                         conv_past[i], i, seq_len)
        conv_states.append(cstate)
        h = _pe(h1, f, p, w["norm_pe"], w["pe_w_gate"], w["pe_w_proj"], w["norm_final"], i)
    y = h.reshape(n_seq, seq_len, D_MODEL)
    gm_v = jnp.stack(gm_rows) if emit_v else None
    return y, jnp.stack(gla_states), jnp.stack(conv_states), gm_v


def _pad_past(past):
    return jnp.pad(past, ((0, 0), (0, 0), (PAST_ROWS - (CONV_W - 1), 0), (0, 0)))


def kernel(x_prompt, x_sample, p_prompt, p_sample, state_gla, state_ffn_conv, norm_mix, norm_ffn, norm_pe, norm_final, gla_w_in, gla_w_gate, gla_b_gate, gla_out_norm, gla_w_o, gm_w_in, gm_b_in, gm_ln_g, gm_ln_b, gm_w_s, gm_b_s, gm_w_o, ffn_w_up, ffn_conv_w, ffn_conv_b, ffn_w_down, pe_w_proj, pe_w_gate):
    n_main = 2 * GLA_HK + 2 * GLA_HV
    w = {
        "norm_mix": norm_mix[:, None, :],
        "norm_ffn": norm_ffn[:, None, :],
        "norm_pe": norm_pe[:, None, :],
        "norm_final": norm_final[None, :],
        "gla_w_main": gla_w_in[:, :, :n_main].astype(BF16),
        "gla_w_gd": jnp.pad(gla_w_in[:, :, n_main:], ((0, 0), (0, 0), (0, V7X_LANES - GLA_RANK))).astype(BF16),
        "gla_w_gate": jnp.pad(gla_w_gate, ((0, 0), (0, V7X_LANES - GLA_RANK), (0, 0))).astype(BF16),
        "gla_b_gate": gla_b_gate[:, None, :],
        "gla_out_norm": gla_out_norm[:, None, :],
        "gla_w_o": gla_w_o.astype(BF16),
        "gm_w_in": gm_w_in.astype(BF16),
        "gm_b_in": gm_b_in[:, None, :],
        "gm_ln_g": gm_ln_g[:, None, :],
        "gm_ln_b": gm_ln_b[:, None, :],
        "gm_w_s": gm_w_s,
        "gm_b_s": gm_b_s,
        "gm_w_o": gm_w_o.astype(BF16),
        "ffn_w_up": ffn_w_up.astype(BF16),
        "ffn_conv_w": ffn_conv_w,
        "ffn_conv_b": ffn_conv_b[:, None, :],
        "ffn_w_down": ffn_w_down.astype(BF16),
        "pe_w_proj": pe_w_proj.astype(BF16),
        "pe_w_gate": pe_w_gate.astype(BF16),
    }
    b_prompt, seq = x_prompt.shape[0], x_prompt.shape[1]
    b_sample, dec_seq = x_sample.shape[0], x_sample.shape[1]
    n_gla = state_gla.shape[0]
    gla_s0_prompt = jnp.zeros((n_gla, b_prompt, GLA_HEADS, GLA_DK, GLA_DV), F32)
    conv_s0_prompt = jnp.zeros((DEPTH, b_prompt, PAST_ROWS, 2 * D_FF), F32)
    y_prompt, gla_state_prompt, conv_state_prompt, _ = _trunk(
        x_prompt, p_prompt, gla_s0_prompt, conv_s0_prompt, w, b_prompt, seq, False)
    y_sample, gla_state_sample, conv_state_sample, gm_v_sample = _trunk(
        x_sample, p_sample, state_gla, _pad_past(state_ffn_conv), w, b_sample, dec_seq, True)
    return (y_prompt, y_sample, gla_state_prompt, gla_state_sample,
            conv_state_prompt, conv_state_sample, gm_v_sample)
```

```python
import functools

import jax
import jax.numpy as jnp
from jax import lax
from jax.experimental import pallas as pl
from jax.experimental.pallas import tpu as pltpu

F32 = jnp.float32
BF16 = jnp.bfloat16

D_MODEL = 2048
DEPTH = 4
GLA_HEADS = 4
GLA_DK = 256
GLA_DV = 512
GLA_HK = GLA_HEADS * GLA_DK
GLA_HV = GLA_HEADS * GLA_DV
GLA_RANK = 16
GLA_TAU = 16.0
GLA_BLOCK = 64
GM_WIDTH = 2048
GM_GROUPS = 8
GM_GROUP_DIM = GM_WIDTH // GM_GROUPS
GM_CHUNK = 128
D_FF = 5632
CONV_W = 3
PE_DIM = 256
EPS = 1e-6

V7X_VMEM_BYTES = 64 * 1024 * 1024
V7X_LANES = 128
V7X_SUBLANES = 8

ARB2 = ("arbitrary", "arbitrary")
ARB1 = ("arbitrary",)


def _params(sem, vmem_bytes):
    assert vmem_bytes <= V7X_VMEM_BYTES - (2 << 20), vmem_bytes
    return pltpu.CompilerParams(dimension_semantics=sem, vmem_limit_bytes=int(vmem_bytes))


def _nbytes(shape, dtype):
    n = 1
    for s in shape:
        n *= s
    return n * jnp.dtype(dtype).itemsize


def _rms(x, g):
    return x * lax.rsqrt(jnp.mean(x * x, axis=-1, keepdims=True) + EPS) * g


def _log_sigmoid(x):
    return jnp.minimum(x, 0.0) - jnp.log(1.0 + jnp.exp(-jnp.abs(x)))


def _dot(a, b):
    return jnp.dot(a, b, preferred_element_type=F32)


def _norm0_kernel(x_ref, g_ref, o_ref):
    o_ref[...] = _rms(x_ref[...], g_ref[...]).astype(BF16)


def _norm0(x, g, layer):
    m = x.shape[0]
    tm = 512
    row = pl.BlockSpec((tm, D_MODEL), lambda i: (i, 0))
    vmem = 2 * (_nbytes((tm, D_MODEL), F32) + _nbytes((tm, D_MODEL), BF16)) + 3 * _nbytes((tm, D_MODEL), F32)
    return pl.pallas_call(
        _norm0_kernel,
        grid=(m // tm,),
        in_specs=[row, pl.BlockSpec((None, 1, D_MODEL), lambda i: (layer, 0, 0))],
        out_specs=row,
        out_shape=jax.ShapeDtypeStruct((m, D_MODEL), BF16),
        compiler_params=_params(ARB1, vmem),
        name="norm0",
    )(x, g)


PROJ_CHUNK_ROWS = 256
PROJ_TN = 1024


def _proj_kernel(*refs, has_bias, act):
    if has_bias:
        x_ref, w_ref, b_ref, o_ref = refs
    else:
        x_ref, w_ref, o_ref = refs
    for r0 in range(0, x_ref.shape[0], PROJ_CHUNK_ROWS):
        rows = slice(r0, r0 + PROJ_CHUNK_ROWS)
        y = _dot(x_ref[rows, :], w_ref[...])
        if has_bias:
            y = y + b_ref[...]
        if act is not None:
            y = act(y)
        o_ref[rows, :] = y.astype(o_ref.dtype)


def _proj(xn, w_tiled, layer, out_dtype, bias=None, act=None, name="proj"):
    m = xn.shape[0]
    tm = 1024
    _, nj, kdim, tn = w_tiled.shape
    in_specs = [
        pl.BlockSpec((tm, kdim), lambda i, j: (i, 0)),
        pl.BlockSpec((None, None, kdim, tn), lambda i, j: (layer, j, 0, 0)),
    ]
    args = [xn, w_tiled]
    if bias is not None:
        in_specs.append(pl.BlockSpec((None, 1, tn), lambda i, j: (layer, 0, j)))
        args.append(bias)
    vmem = (2 * (_nbytes((tm, kdim), BF16) + _nbytes((kdim, tn), BF16) + _nbytes((tm, tn), out_dtype))
            + 4 * _nbytes((tm, tn), F32))
    return pl.pallas_call(
        functools.partial(_proj_kernel, has_bias=bias is not None, act=act),
        grid=(m // tm, nj),
        in_specs=in_specs,
        out_specs=pl.BlockSpec((tm, tn), lambda i, j: (i, j)),
        out_shape=jax.ShapeDtypeStruct((m, nj * tn), out_dtype),
        compiler_params=_params(ARB2, vmem),
        name=name,
    )(*args)


def _gla_gate_kernel(x_ref, wgd_ref, wgate_ref, bgate_ref, o_ref):
    gl = _dot(x_ref[...], wgd_ref[...])
    z = _dot(gl.astype(BF16), wgate_ref[...]) + bgate_ref[...]
    o_ref[...] = _log_sigmoid(z) * (1.0 / GLA_TAU)


def _gla_gate(xn, w_gd, w_gate, b_gate, layer):
    m = xn.shape[0]
    tm = 512
    vmem = (2 * (_nbytes((tm, D_MODEL), BF16) + _nbytes((tm, GLA_HK), F32)
                 + _nbytes((D_MODEL, V7X_LANES), BF16) + _nbytes((V7X_LANES, GLA_HK), BF16))
            + 4 * _nbytes((tm, GLA_HK), F32))
    return pl.pallas_call(
        _gla_gate_kernel,
        grid=(m // tm,),
        in_specs=[
            pl.BlockSpec((tm, D_MODEL), lambda i: (i, 0)),
            pl.BlockSpec((None, D_MODEL, V7X_LANES), lambda i: (layer, 0, 0)),
            pl.BlockSpec((None, V7X_LANES, GLA_HK), lambda i: (layer, 0, 0)),
            pl.BlockSpec((None, 1, GLA_HK), lambda i: (layer, 0, 0)),
        ],
        out_specs=pl.BlockSpec((tm, GLA_HK), lambda i: (i, 0)),
        out_shape=jax.ShapeDtypeStruct((m, GLA_HK), F32),
        compiler_params=_params(ARB1, vmem),
        name="gla_gate",
    )(xn, w_gd, w_gate, b_gate)


def _gla_rec_kernel(q_ref, k_ref, v_ref, r_ref, la_ref, s0_ref, gain_ref,
                    o_ref, s_ref, *, n_blocks):
    blk = GLA_BLOCK

    @pl.when(pl.program_id(1) == 0)
    def _():
        s_ref[...] = s0_ref[...]

    row = lax.broadcasted_iota(jnp.int32, (blk, blk), 0)
    col = lax.broadcasted_iota(jnp.int32, (blk, blk), 1)
    causal = row >= col
    tri = causal.astype(F32).astype(BF16)
    out_g = gain_ref[...]

    def block(n, carry):
        rows = pl.ds(pl.multiple_of(n * blk, blk), blk)
        g = la_ref[rows, :]
        g1 = g.astype(BF16)
        e1 = g - g1.astype(F32)
        g2 = e1.astype(BF16)
        g3 = (e1 - g2.astype(F32)).astype(BF16)
        cum = _dot(tri, g1) + _dot(tri, g2) + _dot(tri, g3)
        last = cum[blk - 1:blk, :]
        q = q_ref[rows, :] * (GLA_DK ** -0.5)
        k = k_ref[rows, :]
        q_dec = (q * jnp.exp(cum)).astype(BF16)
        k_inv = (k * jnp.exp(-cum)).astype(BF16)
        k_end = (k * jnp.exp(last - cum)).astype(BF16)
        decay = jnp.exp(last)
        for hh in range(GLA_HEADS):
            ks = slice(hh * GLA_DK, (hh + 1) * GLA_DK)
            vs = slice(hh * GLA_DV, (hh + 1) * GLA_DV)
            att = lax.dot_general(q_dec[:, ks], k_inv[:, ks], (((1,), (1,)), ((), ())),
                                  preferred_element_type=F32)
            att = jnp.where(causal, att, 0.0)
            vb = v_ref[rows, vs]
            s = s_ref[0, hh]
            o = _dot(att.astype(BF16), vb) + _dot(q_dec[:, ks], s.astype(BF16))
            kv = lax.dot_general(k_end[:, ks], vb, (((0,), (0,)), ((), ())),
                                 preferred_element_type=F32)
            dcol = jnp.broadcast_to(decay[:, ks], (V7X_LANES, GLA_DK)).T
            s_ref[0, hh] = s * jnp.tile(dcol, (1, GLA_DV // V7X_LANES)) + kv
            on = _rms(o, out_g)
            r = r_ref[rows, vs]
            o_ref[rows, vs] = (on * (r * jax.nn.sigmoid(r))).astype(BF16)
        return carry

    lax.fori_loop(0, n_blocks, block, 0)


def _gla_rec(qkr, v, loga, s0, out_g, layer, n_seq, seq_len):
    m = qkr.shape[0]
    tc = min(seq_len, 256)
    nt = seq_len // tc
    row_map = lambda c: (lambda b, t: (b * nt + t, c))
    st_spec = pl.BlockSpec((1, GLA_HEADS, GLA_DK, GLA_DV), lambda b, t: (b, 0, 0, 0))
    s0_spec = pl.BlockSpec((None, 1, GLA_HEADS, GLA_DK, GLA_DV), lambda b, t: (layer, b, 0, 0, 0))
    vmem = (2 * (3 * _nbytes((tc, GLA_HK), F32) + _nbytes((tc, GLA_HV), F32)
                 + 2 * _nbytes((tc, GLA_HV), BF16)
                 + 2 * _nbytes((GLA_HEADS, GLA_DK, GLA_DV), F32))
            + 16 * _nbytes((GLA_BLOCK, GLA_HK), F32) + 6 * _nbytes((GLA_DK, GLA_DV), F32))
    return pl.pallas_call(
        functools.partial(_gla_rec_kernel, n_blocks=tc // GLA_BLOCK),
        grid=(n_seq, nt),
        in_specs=[
            pl.BlockSpec((tc, GLA_HK), row_map(0)),
            pl.BlockSpec((tc, GLA_HK), row_map(1)),
            pl.BlockSpec((tc, GLA_HV), row_map(0)),
            pl.BlockSpec((tc, GLA_HV), row_map(1)),
            pl.BlockSpec((tc, GLA_HK), row_map(0)),
            s0_spec,
            pl.BlockSpec((None, 1, GLA_DV), lambda b, t: (layer, 0, 0)),
        ],
        out_specs=[pl.BlockSpec((tc, GLA_HV), row_map(0)), st_spec],
        out_shape=[jax.ShapeDtypeStruct((m, GLA_HV), BF16),
                   jax.ShapeDtypeStruct((n_seq, GLA_HEADS, GLA_DK, GLA_DV), F32)],
        compiler_params=_params(ARB2, vmem),
        name="gla_rec",
    )(qkr, qkr, v, qkr, loga, s0, out_g)


def _out_proj_kernel(x_ref, w_ref, res_ref, g_ref, h_ref, xn_ref):
    h = res_ref[...] + _dot(x_ref[...], w_ref[...])
    h_ref[...] = h
    xn_ref[...] = _rms(h, g_ref[...]).astype(BF16)


def _out_proj(x, w, res, g_ffn, layer_w, layer_g):
    m = x.shape[0]
    tm = 512
    vmem = (2 * (2 * _nbytes((tm, D_MODEL), BF16) + 2 * _nbytes((tm, D_MODEL), F32)
                 + _nbytes((D_MODEL, D_MODEL), BF16))
            + 3 * _nbytes((tm, D_MODEL), F32))
    row = pl.BlockSpec((tm, D_MODEL), lambda i: (i, 0))
    return pl.pallas_call(
        _out_proj_kernel,
        grid=(m // tm,),
        in_specs=[
            row,
            pl.BlockSpec((None, D_MODEL, D_MODEL), lambda i: (layer_w, 0, 0)),
            row,
            pl.BlockSpec((None, 1, D_MODEL), lambda i: (layer_g, 0, 0)),
        ],
        out_specs=[row, row],
        out_shape=[jax.ShapeDtypeStruct((m, D_MODEL), F32),
                   jax.ShapeDtypeStruct((m, D_MODEL), BF16)],
        compiler_params=_params(ARB1, vmem),
        name="out_proj",
    )(x, w, res, g_ffn)


def _gm_out_kernel(z_ref, lng_ref, lnb_ref, ws_ref, bs_ref, wo_ref, res_ref, g_ref,
                   h_ref, xn_ref, *rest, chunk, n_chunks, emit_v):
    if emit_v:
        vn_ref, gate_ref = rest
    else:
        (gate_ref,) = rest
    v = z_ref[:, GM_WIDTH:]
    mu = jnp.mean(v, axis=-1, keepdims=True)
    vc = v - mu
    var = jnp.mean(vc * vc, axis=-1, keepdims=True)
    vn = vc * lax.rsqrt(var + EPS) * lng_ref[...] + lnb_ref[...]
    if emit_v:
        vn_ref[...] = vn
    vb = vn.astype(BF16)

    row = lax.broadcasted_iota(jnp.int32, (chunk, chunk), 0)
    col = lax.broadcasted_iota(jnp.int32, (chunk, chunk), 1)
    tril = (row >= col).astype(F32)
    for gi in range(GM_GROUPS):
        cs = slice(gi * GM_GROUP_DIM, (gi + 1) * GM_GROUP_DIM)
        wsg = (ws_ref[gi] * tril).astype(BF16)
        bsg = bs_ref[:, gi:gi + 1]
        for c in range(n_chunks):
            rows = slice(c * chunk, (c + 1) * chunk)
            mixed = _dot(wsg, vb[rows, cs]) + bsg
            gate_ref[rows, cs] = (z_ref[rows, cs] * mixed).astype(BF16)

    h = res_ref[...] + _dot(gate_ref[...], wo_ref[...])
    h_ref[...] = h
    xn_ref[...] = _rms(h, g_ref[...]).astype(BF16)


def _gm_out(z, ln_g, ln_b, ws, bs_t, wo, res, g_ffn, layer_m, layer_g, chunk, emit_v):
    m = z.shape[0]
    tm = 256
    row = pl.BlockSpec((tm, D_MODEL), lambda i: (i, 0))
    vec = pl.BlockSpec((None, 1, D_MODEL), lambda i: (layer_m, 0, 0))
    vmem = (2 * (_nbytes((tm, 2 * GM_WIDTH), F32) + 3 * _nbytes((tm, D_MODEL), F32)
                 + _nbytes((tm, D_MODEL), BF16) + _nbytes((D_MODEL, D_MODEL), BF16)
                 + _nbytes((GM_GROUPS, chunk, chunk), F32))
            + 2 * _nbytes((tm, D_MODEL), BF16) + 4 * _nbytes((tm, D_MODEL), F32))
    out_specs = [row, row]
    out_shape = [jax.ShapeDtypeStruct((m, D_MODEL), F32), jax.ShapeDtypeStruct((m, D_MODEL), BF16)]
    if emit_v:
        out_specs.append(row)
        out_shape.append(jax.ShapeDtypeStruct((m, GM_WIDTH), F32))
    return pl.pallas_call(
        functools.partial(_gm_out_kernel, chunk=chunk, n_chunks=tm // chunk, emit_v=emit_v),
        grid=(m // tm,),
        in_specs=[
            pl.BlockSpec((tm, 2 * GM_WIDTH), lambda i: (i, 0)),
            vec,
            vec,
            pl.BlockSpec((GM_GROUPS, chunk, chunk), lambda i: (0, 0, 0)),
            pl.BlockSpec((chunk, GM_GROUPS), lambda i: (0, 0)),
            pl.BlockSpec((None, D_MODEL, D_MODEL), lambda i: (layer_m, 0, 0)),
            row,
            pl.BlockSpec((None, 1, D_MODEL), lambda i: (layer_g, 0, 0)),
        ],
        out_specs=out_specs,
        out_shape=out_shape,
        scratch_shapes=[pltpu.VMEM((tm, GM_WIDTH), BF16)],
        compiler_params=_params(ARB1, vmem),
        name="gm_out",
    )(z, ln_g, ln_b, ws, bs_t, wo, res, g_ffn)


PAST_ROWS = V7X_SUBLANES
FFN_CHUNK_ROWS = 256
FFN_TF = 512


def _ffn_kernel(x_ref, wg_ref, wv_ref, cwg_ref, cwv_ref, cbg_ref, cbv_ref, wd_ref,
                pg_ref, pv_ref, f_ref, csg_ref, csv_ref,
                workg_ref, workv_ref, act_ref, carg_ref, carv_ref,
                *, n_seq, seq_rows, tiles_per_seq, chunk_rows):
    i = pl.program_id(0)
    j = pl.program_id(1)
    stride = seq_rows + PAST_ROWS
    tm = n_seq * seq_rows
    chunks = [(r0, chunk_rows) for r0 in range(0, tm, chunk_rows)]

    def pieces(r0, rows):
        step = min(rows, seq_rows)
        return [(t, step, t + PAST_ROWS * (t // seq_rows + 1)) for t in range(r0, r0 + rows, step)]

    @pl.when(j == 0)
    def _():
        f_ref[...] = jnp.zeros_like(f_ref)

    if tiles_per_seq > 1:
        @pl.when((i == 0) & (j == 0))
        def _():
            carg_ref[...] = jnp.zeros_like(carg_ref)
            carv_ref[...] = jnp.zeros_like(carv_ref)

    halves = ((wg_ref, pg_ref, carg_ref, csg_ref, workg_ref), (wv_ref, pv_ref, carv_ref, csv_ref, workv_ref))
    for w_ref, past_ref, car_ref, cs_ref, work_ref in halves:
        if tiles_per_seq > 1:
            first = (i % tiles_per_seq) == 0
            work_ref[0:PAST_ROWS, :] = jnp.where(first, past_ref[0], car_ref[j])
        else:
            for s in range(n_seq):
                work_ref[s * stride:s * stride + PAST_ROWS, :] = past_ref[s]
    for r0, rows in chunks:
        x = x_ref[r0:r0 + rows, :]
        for w_ref, past_ref, car_ref, cs_ref, work_ref in halves:
            a = _dot(x, w_ref[...])
            for t, n, wrow in pieces(r0, rows):
                work_ref[wrow:wrow + n, :] = a[t - r0:t - r0 + n, :]
    for w_ref, past_ref, car_ref, cs_ref, work_ref in halves:
        for s in range(n_seq):
            cs_ref[s] = work_ref[s * stride + seq_rows:(s + 1) * stride, :]
        if tiles_per_seq > 1:
            car_ref[j] = work_ref[seq_rows:stride, :]

    def conv(work_ref, cw_ref, cb_ref, wrow, n):
        x0 = work_ref[wrow:wrow + n, :]
        x1 = work_ref[wrow - 1:wrow - 1 + n, :]
        x2 = work_ref[wrow - 2:wrow - 2 + n, :]
        return cb_ref[...] + x2 * cw_ref[0:1, :] + x1 * cw_ref[1:2, :] + x0 * cw_ref[2:3, :]

    for r0, rows in chunks:
        for t, n, wrow in pieces(r0, rows):
            cg = conv(workg_ref, cwg_ref, cbg_ref, wrow, n)
            cv = conv(workv_ref, cwv_ref, cbv_ref, wrow, n)
            act_ref[t:t + n, :] = (cg * jax.nn.sigmoid(cg) * cv).astype(BF16)
        f_ref[r0:r0 + rows, :] += _dot(act_ref[r0:r0 + rows, :], wd_ref[...])


def _ffn(xn, w_up, conv_w, conv_b, w_down, past, layer, seq_len):
    m = xn.shape[0]
    tm, tf = 1024, FFN_TF
    nj = D_FF // tf
    if seq_len >= tm:
        n_seq, seq_rows, tiles_per_seq = 1, tm, seq_len // tm
    else:
        n_seq, seq_rows, tiles_per_seq = tm // seq_len, seq_len, 1
    n_tiles = m // tm
    past_map = lambda off: (lambda i, j: (i // tiles_per_seq, 0, j + off))
    cs_spec = pl.BlockSpec((n_seq, PAST_ROWS, tf), lambda i, j: (i, 0, j))
    cs_shape = jax.ShapeDtypeStruct((n_tiles * n_seq, PAST_ROWS, D_FF), F32)
    work = pltpu.VMEM((n_seq * (seq_rows + PAST_ROWS), tf), F32)
    car = pltpu.VMEM((nj, PAST_ROWS, tf), F32)
    vmem = (2 * (_nbytes((tm, D_MODEL), BF16) + 3 * _nbytes((D_MODEL, tf), BF16)
                 + _nbytes((tm, D_MODEL), F32) + 4 * _nbytes((n_seq, PAST_ROWS, tf), F32))
            + 2 * _nbytes((tm + n_seq * PAST_ROWS, tf), F32) + _nbytes((tm, tf), BF16)
            + 2 * _nbytes((nj, PAST_ROWS, tf), F32)
            + 4 * _nbytes((tm, tf), F32) + _nbytes((tm, D_MODEL), F32))
    f, csg, csv = pl.pallas_call(
        functools.partial(_ffn_kernel, n_seq=n_seq, seq_rows=seq_rows, tiles_per_seq=tiles_per_seq,
                          chunk_rows=FFN_CHUNK_ROWS),
        grid=(n_tiles, nj),
        in_specs=[
            pl.BlockSpec((tm, D_MODEL), lambda i, j: (i, 0)),
            pl.BlockSpec((None, None, D_MODEL, tf), lambda i, j: (layer, j, 0, 0)),
            pl.BlockSpec((None, None, D_MODEL, tf), lambda i, j: (layer, j + nj, 0, 0)),
            pl.BlockSpec((None, CONV_W, tf), lambda i, j: (layer, 0, j)),
            pl.BlockSpec((None, CONV_W, tf), lambda i, j: (layer, 0, j + nj)),
            pl.BlockSpec((None, 1, tf), lambda i, j: (layer, 0, j)),
            pl.BlockSpec((None, 1, tf), lambda i, j: (layer, 0, j + nj)),
            pl.BlockSpec((None, tf, D_MODEL), lambda i, j: (layer, j, 0)),
            pl.BlockSpec((n_seq, PAST_ROWS, tf), past_map(0)),
            pl.BlockSpec((n_seq, PAST_ROWS, tf), past_map(nj)),
        ],
        out_specs=[pl.BlockSpec((tm, D_MODEL), lambda i, j: (i, 0)), cs_spec, cs_spec],
        out_shape=[jax.ShapeDtypeStruct((m, D_MODEL), F32), cs_shape, cs_shape],
        scratch_shapes=[work, work, pltpu.VMEM((tm, tf), BF16), car, car],
        compiler_params=_params(ARB2, vmem),
        name="ffn",
    )(xn, w_up, w_up, conv_w, conv_w, conv_b, conv_b, w_down, past, past)
    keep = slice(tiles_per_seq - 1, None, tiles_per_seq)
    cstate = jnp.concatenate([csg[keep], csv[keep]], axis=-1)[:, PAST_ROWS - (CONV_W - 1):, :]
    return f, cstate


def _pe_kernel(h1_ref, f_ref, p_ref, g_ref, wg_ref, wp_ref, gnext_ref, *out_refs, final):
    h2 = h1_ref[...] + f_ref[...]
    hn = _rms(h2, g_ref[...]).astype(BF16)
    gate = jax.nn.sigmoid(_dot(hn, wg_ref[...]))
    proj = _dot(p_ref[...].astype(BF16), wp_ref[...])
    h3 = h2 + gate * proj
    if final:
        (y_ref,) = out_refs
        y_ref[...] = _rms(h3, gnext_ref[...])
    else:
        h_ref, xn_ref = out_refs
        h_ref[...] = h3
        xn_ref[...] = _rms(h3, gnext_ref[...]).astype(BF16)


def _pe(h1, f, p, g_pe, w_gate, w_proj, g_next, layer, layer_next):
    m = h1.shape[0]
    tm = 256
    final = layer_next is None
    row = pl.BlockSpec((tm, D_MODEL), lambda i: (i, 0))
    vmem = (2 * (4 * _nbytes((tm, D_MODEL), F32) + _nbytes((tm, PE_DIM), F32)
                 + _nbytes((D_MODEL, D_MODEL), BF16) + _nbytes((PE_DIM, D_MODEL), BF16))
            + 6 * _nbytes((tm, D_MODEL), F32))
    if final:
        gnext_spec = pl.BlockSpec((1, D_MODEL), lambda i: (0, 0))
        out_specs = [row]
        out_shape = [jax.ShapeDtypeStruct((m, D_MODEL), F32)]
    else:
        gnext_spec = pl.BlockSpec((None, 1, D_MODEL), lambda i: (layer_next, 0, 0))
        out_specs = [row, row]
        out_shape = [jax.ShapeDtypeStruct((m, D_MODEL), F32), jax.ShapeDtypeStruct((m, D_MODEL), BF16)]
    return pl.pallas_call(
        functools.partial(_pe_kernel, final=final),
        grid=(m // tm,),
        in_specs=[
            row,
            row,
            pl.BlockSpec((None, tm, PE_DIM), lambda i: (layer, i, 0)),
            pl.BlockSpec((None, 1, D_MODEL), lambda i: (layer, 0, 0)),
            pl.BlockSpec((None, D_MODEL, D_MODEL), lambda i: (layer, 0, 0)),
            pl.BlockSpec((None, PE_DIM, D_MODEL), lambda i: (layer, 0, 0)),
            gnext_spec,
        ],
        out_specs=out_specs,
        out_shape=out_shape,
        compiler_params=_params(ARB1, vmem),
        name="pe",
    )(h1, f, p, g_pe, w_gate, w_proj, g_next)


def _trunk(x, p, gla_s0, conv_past, w, n_seq, seq_len, emit_v):
    m = n_seq * seq_len
    h = x.reshape(m, D_MODEL)
    p = p.reshape(DEPTH, m, PE_DIM)
    chunk = min(seq_len, GM_CHUNK)
    gla_states, conv_states, gm_rows = [], [], []
    xn = _norm0(h, w["norm_mix"], 0)
    for i in range(DEPTH):
        jm = i // 2
        if i % 2 == 0:
            qkr = _proj(xn, w["gla_w_qkr"], jm, F32, name="gla_qkr")
            v = _proj(xn, w["gla_w_v"], jm, BF16, name="gla_v")
            loga = _gla_gate(xn, w["gla_w_gd"], w["gla_w_gate"], w["gla_b_gate"], jm)
            og, s_fin = _gla_rec(qkr, v, loga, gla_s0, w["gla_out_norm"], jm, n_seq, seq_len)
            gla_states.append(s_fin)
            h1, xnf = _out_proj(og, w["gla_w_o"], h, w["norm_ffn"], jm, i)
        else:
            z = _proj(xn, w["gm_w_in"], jm, F32, bias=w["gm_b_in"], act=jax.nn.gelu, name="gm_in")
            ws = w["gm_w_s"][jm][:, :chunk, :chunk]
            bs_t = w["gm_b_s"][jm][:, :chunk].T
            outs = _gm_out(z, w["gm_ln_g"], w["gm_ln_b"], ws, bs_t, w["gm_w_o"], h,
                           w["norm_ffn"], jm, i, chunk, emit_v)
            h1, xnf = outs[0], outs[1]
            if emit_v:
                gm_rows.append(outs[2].reshape(n_seq, seq_len, GM_WIDTH))
        f, cstate = _ffn(xnf, w["ffn_w_up"], w["ffn_conv_w"], w["ffn_conv_b"], w["ffn_w_down"],
                         conv_past[i], i, seq_len)
        conv_states.append(cstate)
        if i + 1 < DEPTH:
            h, xn = _pe(h1, f, p, w["norm_pe"], w["pe_w_gate"], w["pe_w_proj"], w["norm_mix"], i, i + 1)
        else:
            (y,) = _pe(h1, f, p, w["norm_pe"], w["pe_w_gate"], w["pe_w_proj"], w["norm_final"], i, None)
    y = y.reshape(n_seq, seq_len, D_MODEL)
    gm_v = jnp.stack(gm_rows) if emit_v else None
    return y, jnp.stack(gla_states), jnp.stack(conv_states), gm_v


def _pad_past(past):
    return jnp.pad(past, ((0, 0), (0, 0), (PAST_ROWS - (CONV_W - 1), 0), (0, 0)))


def _tile_cols(w, tn):
    layers, kdim, n = w.shape
    return w.reshape(layers, kdim, n // tn, tn).transpose(0, 2, 1, 3).astype(BF16)


def kernel(x_prompt, x_sample, p_prompt, p_sample, state_gla, state_ffn_conv, norm_mix, norm_ffn, norm_pe, norm_final, gla_w_in, gla_w_gate, gla_b_gate, gla_out_norm, gla_w_o, gm_w_in, gm_b_in, gm_ln_g, gm_ln_b, gm_w_s, gm_b_s, gm_w_o, ffn_w_up, ffn_conv_w, ffn_conv_b, ffn_w_down, pe_w_proj, pe_w_gate):
    qk_end = 2 * GLA_HK
    v_end = qk_end + GLA_HV
    r_end = v_end + GLA_HV
    w_qkr = jnp.concatenate([gla_w_in[:, :, :qk_end], gla_w_in[:, :, v_end:r_end]], axis=-1)
    w = {
        "norm_mix": norm_mix[:, None, :],
        "norm_ffn": norm_ffn[:, None, :],
        "norm_pe": norm_pe[:, None, :],
        "norm_final": norm_final[None, :],
        "gla_w_qkr": _tile_cols(w_qkr, PROJ_TN),
        "gla_w_v": _tile_cols(gla_w_in[:, :, qk_end:v_end], PROJ_TN),
        "gla_w_gd": jnp.pad(gla_w_in[:, :, r_end:], ((0, 0), (0, 0), (0, V7X_LANES - GLA_RANK))).astype(BF16),
        "gla_w_gate": jnp.pad(gla_w_gate, ((0, 0), (0, V7X_LANES - GLA_RANK), (0, 0))).astype(BF16),
        "gla_b_gate": gla_b_gate[:, None, :],
        "gla_out_norm": gla_out_norm[:, None, :],
        "gla_w_o": gla_w_o.astype(BF16),
        "gm_w_in": _tile_cols(gm_w_in, PROJ_TN),
        "gm_b_in": gm_b_in[:, None, :],
        "gm_ln_g": gm_ln_g[:, None, :],
        "gm_ln_b": gm_ln_b[:, None, :],
        "gm_w_s": gm_w_s,
        "gm_b_s": gm_b_s,
        "gm_w_o": gm_w_o.astype(BF16),
        "ffn_w_up": _tile_cols(ffn_w_up, FFN_TF),
        "ffn_conv_w": ffn_conv_w,
        "ffn_conv_b": ffn_conv_b[:, None, :],
        "ffn_w_down": ffn_w_down.astype(BF16),
        "pe_w_proj": pe_w_proj.astype(BF16),
        "pe_w_gate": pe_w_gate.astype(BF16),
    }
    b_prompt, seq = x_prompt.shape[0], x_prompt.shape[1]
    b_sample, dec_seq = x_sample.shape[0], x_sample.shape[1]
    n_gla = state_gla.shape[0]
    gla_s0_prompt = jnp.zeros((n_gla, b_prompt, GLA_HEADS, GLA_DK, GLA_DV), F32)
    conv_s0_prompt = jnp.zeros((DEPTH, b_prompt, PAST_ROWS, 2 * D_FF), F32)
    y_prompt, gla_state_prompt, conv_state_prompt, _ = _trunk(
        x_prompt, p_prompt, gla_s0_prompt, conv_s0_prompt, w, b_prompt, seq, False)
    y_sample, gla_state_sample, conv_state_sample, gm_v_sample = _trunk(
        x_sample, p_sample, state_gla, _pad_past(state_ffn_conv), w, b_sample, dec_seq, True)
    return (y_prompt, y_sample, gla_state_prompt, gla_state_sample,
            conv_state_prompt, conv_state_sample, gm_v_sample)
```

```python
import functools

import jax
import jax.numpy as jnp
from jax import lax
from jax.experimental import pallas as pl
from jax.experimental.pallas import tpu as pltpu

F32 = jnp.float32
BF16 = jnp.bfloat16

D_MODEL = 2048
DEPTH = 4
GLA_HEADS = 4
GLA_DK = 256
GLA_DV = 512
GLA_HK = GLA_HEADS * GLA_DK
GLA_HV = GLA_HEADS * GLA_DV
GLA_RANK = 16
GLA_TAU = 16.0
GLA_BLOCK = 64
GM_WIDTH = 2048
GM_GROUPS = 8
GM_GROUP_DIM = GM_WIDTH // GM_GROUPS
GM_CHUNK = 128
D_FF = 5632
CONV_W = 3
PE_DIM = 256
EPS = 1e-6

V7X_VMEM_BYTES = 64 * 1024 * 1024
V7X_LANES = 128
V7X_SUBLANES = 8

ARB2 = ("arbitrary", "arbitrary")
ARB1 = ("arbitrary",)


def _params(sem, vmem_bytes):
    assert vmem_bytes <= V7X_VMEM_BYTES - (2 << 20), vmem_bytes
    return pltpu.CompilerParams(dimension_semantics=sem, vmem_limit_bytes=int(vmem_bytes))


def _nbytes(shape, dtype):
    n = 1
    for s in shape:
        n *= s
    return n * jnp.dtype(dtype).itemsize


def _rms(x, g):
    return x * lax.rsqrt(jnp.mean(x * x, axis=-1, keepdims=True) + EPS) * g


def _log_sigmoid(x):
    return jnp.minimum(x, 0.0) - jnp.log(1.0 + jnp.exp(-jnp.abs(x)))


def _dot(a, b):
    return jnp.dot(a, b, preferred_element_type=F32)


def _norm0_kernel(x_ref, g_ref, o_ref):
    o_ref[...] = _rms(x_ref[...], g_ref[...]).astype(BF16)


def _norm0(x, g, layer):
    m = x.shape[0]
    tm = 512
    row = pl.BlockSpec((tm, D_MODEL), lambda i: (i, 0))
    vmem = 2 * (_nbytes((tm, D_MODEL), F32) + _nbytes((tm, D_MODEL), BF16)) + 3 * _nbytes((tm, D_MODEL), F32)
    return pl.pallas_call(
        _norm0_kernel,
        grid=(m // tm,),
        in_specs=[row, pl.BlockSpec((None, 1, D_MODEL), lambda i: (layer, 0, 0))],
        out_specs=row,
        out_shape=jax.ShapeDtypeStruct((m, D_MODEL), BF16),
        compiler_params=_params(ARB1, vmem),
        name="norm0",
    )(x, g)


PROJ_CHUNK_ROWS = 128
PROJ_TN = 1024


def _proj_kernel(*refs, has_bias, act, w_is_nk):
    if has_bias:
        x_ref, w_ref, b_ref, o_ref, wb_ref = refs
    else:
        x_ref, w_ref, o_ref, wb_ref = refs

    @pl.when(pl.program_id(1) == 0)
    def _():
        w = w_ref[...]
        wb_ref[...] = (w.T if w_is_nk else w).astype(BF16)

    chunks = [slice(r0, r0 + PROJ_CHUNK_ROWS) for r0 in range(0, x_ref.shape[0], PROJ_CHUNK_ROWS)]
    ys = [_dot(x_ref[rows, :], wb_ref[...]) for rows in chunks]
    for rows, y in zip(chunks, ys):
        if has_bias:
            y = y + b_ref[...]
        if act is not None:
            y = act(y)
        o_ref[rows, :] = y.astype(o_ref.dtype)


def _proj(xn, w, layer, out_dtype, nj, col_tile=lambda j: j, bias=None, act=None, w_is_nk=False, name="proj"):
    m = xn.shape[0]
    tm, tn = 1024, PROJ_TN
    kdim = xn.shape[1]
    if w_is_nk:
        w_block, w_map = (None, tn, kdim), (lambda j, i: (layer, col_tile(j), 0))
    else:
        w_block, w_map = (None, kdim, tn), (lambda j, i: (layer, 0, col_tile(j)))
    in_specs = [pl.BlockSpec((tm, kdim), lambda j, i: (i, 0)), pl.BlockSpec(w_block, w_map)]
    args = [xn, w]
    if bias is not None:
        in_specs.append(pl.BlockSpec((None, 1, tn), lambda j, i: (layer, 0, j)))
        args.append(bias)
    vmem = (2 * (_nbytes((tm, kdim), BF16) + _nbytes((kdim, tn), F32) + _nbytes((tm, tn), out_dtype))
            + _nbytes((kdim, tn), BF16) + 4 * _nbytes((tm, tn), F32))
    return pl.pallas_call(
        functools.partial(_proj_kernel, has_bias=bias is not None, act=act, w_is_nk=w_is_nk),
        grid=(nj, m // tm),
        in_specs=in_specs,
        out_specs=pl.BlockSpec((tm, tn), lambda j, i: (i, j)),
        out_shape=jax.ShapeDtypeStruct((m, nj * tn), out_dtype),
        scratch_shapes=[pltpu.VMEM((kdim, tn), BF16)],
        compiler_params=_params(ARB2, vmem),
        name=name,
    )(*args)


def _gla_gate_kernel(x_ref, wgd_ref, wgate_ref, bgate_ref, o_ref):
    gl = _dot(x_ref[...], wgd_ref[...])
    z = _dot(gl.astype(BF16), wgate_ref[...]) + bgate_ref[...]
    o_ref[...] = _log_sigmoid(z) * (1.0 / GLA_TAU)


def _gla_gate(xn, w_gd, w_gate, b_gate, layer):
    m = xn.shape[0]
    tm = 512
    vmem = (2 * (_nbytes((tm, D_MODEL), BF16) + _nbytes((tm, GLA_HK), F32)
                 + _nbytes((D_MODEL, V7X_LANES), BF16) + _nbytes((V7X_LANES, GLA_HK), BF16))
            + 4 * _nbytes((tm, GLA_HK), F32))
    return pl.pallas_call(
        _gla_gate_kernel,
        grid=(m // tm,),
        in_specs=[
            pl.BlockSpec((tm, D_MODEL), lambda i: (i, 0)),
            pl.BlockSpec((None, D_MODEL, V7X_LANES), lambda i: (layer, 0, 0)),
            pl.BlockSpec((None, V7X_LANES, GLA_HK), lambda i: (layer, 0, 0)),
            pl.BlockSpec((None, 1, GLA_HK), lambda i: (layer, 0, 0)),
        ],
        out_specs=pl.BlockSpec((tm, GLA_HK), lambda i: (i, 0)),
        out_shape=jax.ShapeDtypeStruct((m, GLA_HK), F32),
        compiler_params=_params(ARB1, vmem),
        name="gla_gate",
    )(xn, w_gd, w_gate, b_gate)


def _gla_rec_kernel(q_ref, k_ref, v_ref, r_ref, la_ref, s0_ref, gain_ref,
                    o_ref, s_ref, *, n_blocks):
    blk = GLA_BLOCK

    @pl.when(pl.program_id(1) == 0)
    def _():
        s_ref[...] = s0_ref[...]

    row = lax.broadcasted_iota(jnp.int32, (blk, blk), 0)
    col = lax.broadcasted_iota(jnp.int32, (blk, blk), 1)
    causal = row >= col
    tri = causal.astype(F32).astype(BF16)
    out_g = gain_ref[...]

    heads = range(GLA_HEADS)
    ksl = [slice(hh * GLA_DK, (hh + 1) * GLA_DK) for hh in heads]
    vsl = [slice(hh * GLA_DV, (hh + 1) * GLA_DV) for hh in heads]
    rsl = [slice(n * blk, (n + 1) * blk) for n in range(n_blocks)]
    q_dec, k_inv, k_end, decay = [], [], [], []
    for n in range(n_blocks):
        g = la_ref[rsl[n], :]
        g1 = g.astype(BF16)
        e1 = g - g1.astype(F32)
        g2 = e1.astype(BF16)
        g3 = (e1 - g2.astype(F32)).astype(BF16)
        cum = _dot(tri, g1) + _dot(tri, g2) + _dot(tri, g3)
        last = cum[blk - 1:blk, :]
        q = q_ref[rsl[n], :] * (GLA_DK ** -0.5)
        k = k_ref[rsl[n], :]
        q_dec.append((q * jnp.exp(cum)).astype(BF16))
        k_inv.append((k * jnp.exp(-cum)).astype(BF16))
        k_end.append((k * jnp.exp(last - cum)).astype(BF16))
        decay.append(jnp.exp(last))
    o_intra, kv = {}, {}
    for n in range(n_blocks):
        for hh in heads:
            att = lax.dot_general(q_dec[n][:, ksl[hh]], k_inv[n][:, ksl[hh]], (((1,), (1,)), ((), ())),
                                  preferred_element_type=F32)
            att = jnp.where(causal, att, 0.0)
            vb = v_ref[rsl[n], vsl[hh]]
            o_intra[n, hh] = _dot(att.astype(BF16), vb)
            kv[n, hh] = lax.dot_general(k_end[n][:, ksl[hh]], vb, (((0,), (0,)), ((), ())),
                                        preferred_element_type=F32)
    s = [s_ref[0, hh] for hh in heads]
    for n in range(n_blocks):
        for hh in heads:
            o = o_intra[n, hh] + _dot(q_dec[n][:, ksl[hh]], s[hh].astype(BF16))
            dcol = jnp.broadcast_to(decay[n][:, ksl[hh]], (V7X_LANES, GLA_DK)).T
            s[hh] = s[hh] * jnp.tile(dcol, (1, GLA_DV // V7X_LANES)) + kv[n, hh]
            on = _rms(o, out_g)
            r = r_ref[rsl[n], vsl[hh]]
            o_ref[rsl[n], vsl[hh]] = (on * (r * jax.nn.sigmoid(r))).astype(BF16)
    for hh in heads:
        s_ref[0, hh] = s[hh]


def _gla_rec(qkr, v, loga, s0, out_g, layer, n_seq, seq_len):
    m = qkr.shape[0]
    tc = min(seq_len, 256)
    nt = seq_len // tc
    row_map = lambda c: (lambda b, t: (b * nt + t, c))
    st_spec = pl.BlockSpec((1, GLA_HEADS, GLA_DK, GLA_DV), lambda b, t: (b, 0, 0, 0))
    s0_spec = pl.BlockSpec((None, 1, GLA_HEADS, GLA_DK, GLA_DV), lambda b, t: (layer, b, 0, 0, 0))
    vmem = (2 * (3 * _nbytes((tc, GLA_HK), F32) + _nbytes((tc, GLA_HV), F32)
                 + 2 * _nbytes((tc, GLA_HV), BF16)
                 + 2 * _nbytes((GLA_HEADS, GLA_DK, GLA_DV), F32))
            + (tc // GLA_BLOCK) * (8 * _nbytes((GLA_BLOCK, GLA_HK), F32) + _nbytes((GLA_BLOCK, GLA_HV), F32)
                                   + _nbytes((GLA_HEADS, GLA_DK, GLA_DV), F32))
            + 8 * _nbytes((GLA_DK, GLA_DV), F32))
    return pl.pallas_call(
        functools.partial(_gla_rec_kernel, n_blocks=tc // GLA_BLOCK),
        grid=(n_seq, nt),
        in_specs=[
            pl.BlockSpec((tc, GLA_HK), row_map(0)),
            pl.BlockSpec((tc, GLA_HK), row_map(1)),
            pl.BlockSpec((tc, GLA_HV), row_map(0)),
            pl.BlockSpec((tc, GLA_HV), row_map(1)),
            pl.BlockSpec((tc, GLA_HK), row_map(0)),
            s0_spec,
            pl.BlockSpec((None, 1, GLA_DV), lambda b, t: (layer, 0, 0)),
        ],
        out_specs=[pl.BlockSpec((tc, GLA_HV), row_map(0)), st_spec],
        out_shape=[jax.ShapeDtypeStruct((m, GLA_HV), BF16),
                   jax.ShapeDtypeStruct((n_seq, GLA_HEADS, GLA_DK, GLA_DV), F32)],
        compiler_params=_params(ARB2, vmem),
        name="gla_rec",
    )(qkr, qkr, v, qkr, loga, s0, out_g)


def _out_proj_kernel(x_ref, w_ref, res_ref, g_ref, h_ref, xn_ref):
    h = res_ref[...] + _dot(x_ref[...], w_ref[...])
    h_ref[...] = h
    xn_ref[...] = _rms(h, g_ref[...]).astype(BF16)


def _out_proj(x, w, res, g_ffn, layer_w, layer_g):
    m = x.shape[0]
    tm = 512
    vmem = (2 * (2 * _nbytes((tm, D_MODEL), BF16) + 2 * _nbytes((tm, D_MODEL), F32)
                 + _nbytes((D_MODEL, D_MODEL), BF16))
            + 3 * _nbytes((tm, D_MODEL), F32))
    row = pl.BlockSpec((tm, D_MODEL), lambda i: (i, 0))
    return pl.pallas_call(
        _out_proj_kernel,
        grid=(m // tm,),
        in_specs=[
            row,
            pl.BlockSpec((None, D_MODEL, D_MODEL), lambda i: (layer_w, 0, 0)),
            row,
            pl.BlockSpec((None, 1, D_MODEL), lambda i: (layer_g, 0, 0)),
        ],
        out_specs=[row, row],
        out_shape=[jax.ShapeDtypeStruct((m, D_MODEL), F32),
                   jax.ShapeDtypeStruct((m, D_MODEL), BF16)],
        compiler_params=_params(ARB1, vmem),
        name="out_proj",
    )(x, w, res, g_ffn)


def _gm_out_kernel(z_ref, lng_ref, lnb_ref, ws_ref, bs_ref, wo_ref, res_ref, g_ref,
                   h_ref, xn_ref, *rest, chunk, n_chunks, emit_v):
    if emit_v:
        vn_ref, gate_ref = rest
    else:
        (gate_ref,) = rest
    v = z_ref[:, GM_WIDTH:]
    mu = jnp.mean(v, axis=-1, keepdims=True)
    vc = v - mu
    var = jnp.mean(vc * vc, axis=-1, keepdims=True)
    vn = vc * lax.rsqrt(var + EPS) * lng_ref[...] + lnb_ref[...]
    if emit_v:
        vn_ref[...] = vn
    vb = vn.astype(BF16)

    row = lax.broadcasted_iota(jnp.int32, (chunk, chunk), 0)
    col = lax.broadcasted_iota(jnp.int32, (chunk, chunk), 1)
    tril = (row >= col).astype(F32)
    for gi in range(GM_GROUPS):
        cs = slice(gi * GM_GROUP_DIM, (gi + 1) * GM_GROUP_DIM)
        wsg = (ws_ref[gi] * tril).astype(BF16)
        bsg = bs_ref[:, gi:gi + 1]
        for c in range(n_chunks):
            rows = slice(c * chunk, (c + 1) * chunk)
            mixed = _dot(wsg, vb[rows, cs]) + bsg
            gate_ref[rows, cs] = (z_ref[rows, cs] * mixed).astype(BF16)

    h = res_ref[...] + _dot(gate_ref[...], wo_ref[...])
    h_ref[...] = h
    xn_ref[...] = _rms(h, g_ref[...]).astype(BF16)


def _gm_out(z, ln_g, ln_b, ws, bs_t, wo, res, g_ffn, layer_m, layer_g, chunk, emit_v):
    m = z.shape[0]
    tm = 256
    row = pl.BlockSpec((tm, D_MODEL), lambda i: (i, 0))
    vec = pl.BlockSpec((None, 1, D_MODEL), lambda i: (layer_m, 0, 0))
    vmem = (2 * (_nbytes((tm, 2 * GM_WIDTH), F32) + 3 * _nbytes((tm, D_MODEL), F32)
                 + _nbytes((tm, D_MODEL), BF16) + _nbytes((D_MODEL, D_MODEL), BF16)
                 + _nbytes((GM_GROUPS, chunk, chunk), F32))
            + 2 * _nbytes((tm, D_MODEL), BF16) + 4 * _nbytes((tm, D_MODEL), F32))
    out_specs = [row, row]
    out_shape = [jax.ShapeDtypeStruct((m, D_MODEL), F32), jax.ShapeDtypeStruct((m, D_MODEL), BF16)]
    if emit_v:
        out_specs.append(row)
        out_shape.append(jax.ShapeDtypeStruct((m, GM_WIDTH), F32))
    return pl.pallas_call(
        functools.partial(_gm_out_kernel, chunk=chunk, n_chunks=tm // chunk, emit_v=emit_v),
        grid=(m // tm,),
        in_specs=[
            pl.BlockSpec((tm, 2 * GM_WIDTH), lambda i: (i, 0)),
            vec,
            vec,
            pl.BlockSpec((GM_GROUPS, chunk, chunk), lambda i: (0, 0, 0)),
            pl.BlockSpec((chunk, GM_GROUPS), lambda i: (0, 0)),
            pl.BlockSpec((None, D_MODEL, D_MODEL), lambda i: (layer_m, 0, 0)),
            row,
            pl.BlockSpec((None, 1, D_MODEL), lambda i: (layer_g, 0, 0)),
        ],
        out_specs=out_specs,
        out_shape=out_shape,
        scratch_shapes=[pltpu.VMEM((tm, GM_WIDTH), BF16)],
        compiler_params=_params(ARB1, vmem),
        name="gm_out",
    )(z, ln_g, ln_b, ws, bs_t, wo, res, g_ffn)


PAST_ROWS = V7X_SUBLANES
FFN_CHUNK_ROWS = 512
FFN_TF = 512


def _ffn_kernel(x_ref, wg_ref, wv_ref, cwg_ref, cwv_ref, cbg_ref, cbv_ref, wd_ref,
                pg_ref, pv_ref, f_ref, csg_ref, csv_ref,
                workg_ref, workv_ref, act_ref, carg_ref, carv_ref,
                *, n_seq, seq_rows, tiles_per_seq, chunk_rows):
    i = pl.program_id(0)
    j = pl.program_id(1)
    stride = seq_rows + PAST_ROWS
    tm = n_seq * seq_rows
    chunks = [(r0, chunk_rows) for r0 in range(0, tm, chunk_rows)]

    def pieces(r0, rows):
        step = min(rows, seq_rows)
        return [(t, step, t + PAST_ROWS * (t // seq_rows + 1)) for t in range(r0, r0 + rows, step)]

    @pl.when(j == 0)
    def _():
        f_ref[...] = jnp.zeros_like(f_ref)

    if tiles_per_seq > 1:
        @pl.when((i == 0) & (j == 0))
        def _():
            carg_ref[...] = jnp.zeros_like(carg_ref)
            carv_ref[...] = jnp.zeros_like(carv_ref)

    halves = ((wg_ref, pg_ref, carg_ref, csg_ref, workg_ref), (wv_ref, pv_ref, carv_ref, csv_ref, workv_ref))
    for w_ref, past_ref, car_ref, cs_ref, work_ref in halves:
        if tiles_per_seq > 1:
            first = (i % tiles_per_seq) == 0
            work_ref[0:PAST_ROWS, :] = jnp.where(first, past_ref[0], car_ref[j])
        else:
            for s in range(n_seq):
                work_ref[s * stride:s * stride + PAST_ROWS, :] = past_ref[s]
    for r0, rows in chunks:
        for w_ref, past_ref, car_ref, cs_ref, work_ref in halves:
            a = _dot(x_ref[r0:r0 + rows, :], w_ref[...])
            for t, n, wrow in pieces(r0, rows):
                work_ref[wrow:wrow + n, :] = a[t - r0:t - r0 + n, :]
    for w_ref, past_ref, car_ref, cs_ref, work_ref in halves:
        for s in range(n_seq):
            cs_ref[s] = work_ref[s * stride + seq_rows:(s + 1) * stride, :]
        if tiles_per_seq > 1:
            car_ref[j] = work_ref[seq_rows:stride, :]

    def conv(work_ref, cw_ref, cb_ref, wrow, n):
        x0 = work_ref[wrow:wrow + n, :]
        x1 = work_ref[wrow - 1:wrow - 1 + n, :]
        x2 = work_ref[wrow - 2:wrow - 2 + n, :]
        return cb_ref[...] + x2 * cw_ref[0:1, :] + x1 * cw_ref[1:2, :] + x0 * cw_ref[2:3, :]

    for r0, rows in chunks:
        for t, n, wrow in pieces(r0, rows):
            cg = conv(workg_ref, cwg_ref, cbg_ref, wrow, n)
            cv = conv(workv_ref, cwv_ref, cbv_ref, wrow, n)
            act_ref[t:t + n, :] = (cg * jax.nn.sigmoid(cg) * cv).astype(BF16)
        f_ref[r0:r0 + rows, :] += _dot(act_ref[r0:r0 + rows, :], wd_ref[...])


def _ffn(xn, w_up, conv_w, conv_b, w_down, past, layer, seq_len):
    m = xn.shape[0]
    tm, tf = 1024, FFN_TF
    nj = D_FF // tf
    if seq_len >= tm:
        n_seq, seq_rows, tiles_per_seq = 1, tm, seq_len // tm
    else:
        n_seq, seq_rows, tiles_per_seq = tm // seq_len, seq_len, 1
    n_tiles = m // tm
    past_map = lambda off: (lambda i, j: (i // tiles_per_seq, 0, j + off))
    cs_spec = pl.BlockSpec((n_seq, PAST_ROWS, tf), lambda i, j: (i, 0, j))
    cs_shape = jax.ShapeDtypeStruct((n_tiles * n_seq, PAST_ROWS, D_FF), F32)
    work = pltpu.VMEM((n_seq * (seq_rows + PAST_ROWS), tf), F32)
    car = pltpu.VMEM((nj, PAST_ROWS, tf), F32)
    vmem = (2 * (_nbytes((tm, D_MODEL), BF16) + 3 * _nbytes((D_MODEL, tf), BF16)
                 + _nbytes((tm, D_MODEL), F32) + 4 * _nbytes((n_seq, PAST_ROWS, tf), F32))
            + 2 * _nbytes((tm + n_seq * PAST_ROWS, tf), F32) + _nbytes((tm, tf), BF16)
            + 2 * _nbytes((nj, PAST_ROWS, tf), F32)
            + 4 * _nbytes((tm, tf), F32) + _nbytes((tm, D_MODEL), F32))
    f, csg, csv = pl.pallas_call(
        functools.partial(_ffn_kernel, n_seq=n_seq, seq_rows=seq_rows, tiles_per_seq=tiles_per_seq,
                          chunk_rows=FFN_CHUNK_ROWS),
        grid=(n_tiles, nj),
        in_specs=[
            pl.BlockSpec((tm, D_MODEL), lambda i, j: (i, 0)),
            pl.BlockSpec((None, D_MODEL, tf), lambda i, j: (layer, 0, j)),
            pl.BlockSpec((None, D_MODEL, tf), lambda i, j: (layer, 0, j + nj)),
            pl.BlockSpec((None, CONV_W, tf), lambda i, j: (layer, 0, j)),
            pl.BlockSpec((None, CONV_W, tf), lambda i, j: (layer, 0, j + nj)),
            pl.BlockSpec((None, 1, tf), lambda i, j: (layer, 0, j)),
            pl.BlockSpec((None, 1, tf), lambda i, j: (layer, 0, j + nj)),
            pl.BlockSpec((None, tf, D_MODEL), lambda i, j: (layer, j, 0)),
            pl.BlockSpec((n_seq, PAST_ROWS, tf), past_map(0)),
            pl.BlockSpec((n_seq, PAST_ROWS, tf), past_map(nj)),
        ],
        out_specs=[pl.BlockSpec((tm, D_MODEL), lambda i, j: (i, 0)), cs_spec, cs_spec],
        out_shape=[jax.ShapeDtypeStruct((m, D_MODEL), F32), cs_shape, cs_shape],
        scratch_shapes=[work, work, pltpu.VMEM((tm, tf), BF16), car, car],
        compiler_params=_params(ARB2, vmem),
        name="ffn",
    )(xn, w_up, w_up, conv_w, conv_w, conv_b, conv_b, w_down, past, past)
    keep = slice(tiles_per_seq - 1, None, tiles_per_seq)
    cstate = jnp.concatenate([csg[keep], csv[keep]], axis=-1)[:, PAST_ROWS - (CONV_W - 1):, :]
    return f, cstate


def _pe_kernel(h1_ref, f_ref, p_ref, g_ref, wg_ref, wp_ref, gnext_ref, *out_refs, final):
    h2 = h1_ref[...] + f_ref[...]
    hn = _rms(h2, g_ref[...]).astype(BF16)
    gate = jax.nn.sigmoid(_dot(hn, wg_ref[...]))
    proj = _dot(p_ref[...].astype(BF16), wp_ref[...])
    h3 = h2 + gate * proj
    if final:
        (y_ref,) = out_refs
        y_ref[...] = _rms(h3, gnext_ref[...])
    else:
        h_ref, xn_ref = out_refs
        h_ref[...] = h3
        xn_ref[...] = _rms(h3, gnext_ref[...]).astype(BF16)


def _pe(h1, f, p, g_pe, w_gate, w_proj, g_next, layer, layer_next):
    m = h1.shape[0]
    tm = 256
    final = layer_next is None
    row = pl.BlockSpec((tm, D_MODEL), lambda i: (i, 0))
    vmem = (2 * (4 * _nbytes((tm, D_MODEL), F32) + _nbytes((tm, PE_DIM), F32)
                 + _nbytes((D_MODEL, D_MODEL), BF16) + _nbytes((PE_DIM, D_MODEL), BF16))
            + 6 * _nbytes((tm, D_MODEL), F32))
    if final:
        gnext_spec = pl.BlockSpec((1, D_MODEL), lambda i: (0, 0))
        out_specs = [row]
        out_shape = [jax.ShapeDtypeStruct((m, D_MODEL), F32)]
    else:
        gnext_spec = pl.BlockSpec((None, 1, D_MODEL), lambda i: (layer_next, 0, 0))
        out_specs = [row, row]
        out_shape = [jax.ShapeDtypeStruct((m, D_MODEL), F32), jax.ShapeDtypeStruct((m, D_MODEL), BF16)]
    return pl.pallas_call(
        functools.partial(_pe_kernel, final=final),
        grid=(m // tm,),
        in_specs=[
            row,
            row,
            pl.BlockSpec((None, tm, PE_DIM), lambda i: (layer, i, 0)),
            pl.BlockSpec((None, 1, D_MODEL), lambda i: (layer, 0, 0)),
            pl.BlockSpec((None, D_MODEL, D_MODEL), lambda i: (layer, 0, 0)),
            pl.BlockSpec((None, PE_DIM, D_MODEL), lambda i: (layer, 0, 0)),
            gnext_spec,
        ],
        out_specs=out_specs,
        out_shape=out_shape,
        compiler_params=_params(ARB1, vmem),
        name="pe",
    )(h1, f, p, g_pe, w_gate, w_proj, g_next)


def _trunk(x, p, gla_s0, conv_past, w, n_seq, seq_len, emit_v):
    m = n_seq * seq_len
    h = x.reshape(m, D_MODEL)
    p = p.reshape(DEPTH, m, PE_DIM)
    chunk = min(seq_len, GM_CHUNK)
    gla_states, conv_states, gm_rows = [], [], []
    xn = _norm0(h, w["norm_mix"], 0)
    for i in range(DEPTH):
        jm = i // 2
        if i % 2 == 0:
            qkr = _proj(xn, w["gla_w_in_t"], jm, F32, 4, col_tile=lambda j: j + 2 * (j // 2), w_is_nk=True,
                        name="gla_qkr")
            v = _proj(xn, w["gla_w_in_t"], jm, BF16, 2, col_tile=lambda j: j + 2, w_is_nk=True, name="gla_v")
            loga = _gla_gate(xn, w["gla_w_gd"], w["gla_w_gate"], w["gla_b_gate"], jm)
            og, s_fin = _gla_rec(qkr, v, loga, gla_s0, w["gla_out_norm"], jm, n_seq, seq_len)
            gla_states.append(s_fin)
            h1, xnf = _out_proj(og, w["gla_w_o"], h, w["norm_ffn"], jm, i)
        else:
            z = _proj(xn, w["gm_w_in"], jm, F32, 2 * GM_WIDTH // PROJ_TN, bias=w["gm_b_in"],
                      act=jax.nn.gelu, name="gm_in")
            ws = w["gm_w_s"][jm][:, :chunk, :chunk]
            bs_t = w["gm_b_s"][jm][:, :chunk].T
            outs = _gm_out(z, w["gm_ln_g"], w["gm_ln_b"], ws, bs_t, w["gm_w_o"], h,
                           w["norm_ffn"], jm, i, chunk, emit_v)
            h1, xnf = outs[0], outs[1]
            if emit_v:
                gm_rows.append(outs[2].reshape(n_seq, seq_len, GM_WIDTH))
        f, cstate = _ffn(xnf, w["ffn_w_up"], w["ffn_conv_w"], w["ffn_conv_b"], w["ffn_w_down"],
                         conv_past[i], i, seq_len)
        conv_states.append(cstate)
        if i + 1 < DEPTH:
            h, xn = _pe(h1, f, p, w["norm_pe"], w["pe_w_gate"], w["pe_w_proj"], w["norm_mix"], i, i + 1)
        else:
            (y,) = _pe(h1, f, p, w["norm_pe"], w["pe_w_gate"], w["pe_w_proj"], w["norm_final"], i, None)
    y = y.reshape(n_seq, seq_len, D_MODEL)
    gm_v = jnp.stack(gm_rows) if emit_v else None
    return y, jnp.stack(gla_states), jnp.stack(conv_states), gm_v


def _pad_past(past):
    return jnp.pad(past, ((0, 0), (0, 0), (PAST_ROWS - (CONV_W - 1), 0), (0, 0)))


def kernel(x_prompt, x_sample, p_prompt, p_sample, state_gla, state_ffn_conv, norm_mix, norm_ffn, norm_pe, norm_final, gla_w_in, gla_w_gate, gla_b_gate, gla_out_norm, gla_w_o, gm_w_in, gm_b_in, gm_ln_g, gm_ln_b, gm_w_s, gm_b_s, gm_w_o, ffn_w_up, ffn_conv_w, ffn_conv_b, ffn_w_down, pe_w_proj, pe_w_gate):
    r_end = 2 * GLA_HK + 2 * GLA_HV
    w = {
        "norm_mix": norm_mix[:, None, :],
        "norm_ffn": norm_ffn[:, None, :],
        "norm_pe": norm_pe[:, None, :],
        "norm_final": norm_final[None, :],
        "gla_w_in_t": jnp.swapaxes(gla_w_in, 1, 2),
        "gla_w_gd": jnp.pad(gla_w_in[:, :, r_end:], ((0, 0), (0, 0), (0, V7X_LANES - GLA_RANK))).astype(BF16),
        "gla_w_gate": jnp.pad(gla_w_gate, ((0, 0), (0, V7X_LANES - GLA_RANK), (0, 0))).astype(BF16),
        "gla_b_gate": gla_b_gate[:, None, :],
        "gla_out_norm": gla_out_norm[:, None, :],
        "gla_w_o": gla_w_o.astype(BF16),
        "gm_w_in": gm_w_in,
        "gm_b_in": gm_b_in[:, None, :],
        "gm_ln_g": gm_ln_g[:, None, :],
        "gm_ln_b": gm_ln_b[:, None, :],
        "gm_w_s": gm_w_s,
        "gm_b_s": gm_b_s,
        "gm_w_o": gm_w_o.astype(BF16),
        "ffn_w_up": ffn_w_up.astype(BF16),
        "ffn_conv_w": ffn_conv_w,
        "ffn_conv_b": ffn_conv_b[:, None, :],
        "ffn_w_down": ffn_w_down.astype(BF16),
        "pe_w_proj": pe_w_proj.astype(BF16),
        "pe_w_gate": pe_w_gate.astype(BF16),
    }
    b_prompt, seq = x_prompt.shape[0], x_prompt.shape[1]
    b_sample, dec_seq = x_sample.shape[0], x_sample.shape[1]
    n_gla = state_gla.shape[0]
    gla_s0_prompt = jnp.zeros((n_gla, b_prompt, GLA_HEADS, GLA_DK, GLA_DV), F32)
    conv_s0_prompt = jnp.zeros((DEPTH, b_prompt, PAST_ROWS, 2 * D_FF), F32)
    y_prompt, gla_state_prompt, conv_state_prompt, _ = _trunk(
        x_prompt, p_prompt, gla_s0_prompt, conv_s0_prompt, w, b_prompt, seq, False)
    y_sample, gla_state_sample, conv_state_sample, gm_v_sample = _trunk(
        x_sample, p_sample, state_gla, _pad_past(state_ffn_conv), w, b_sample, dec_seq, True)
    return (y_prompt, y_sample, gla_state_prompt, gla_state_sample,
            conv_state_prompt, conv_state_sample, gm_v_sample)
```

```python
import functools

import jax
import jax.numpy as jnp
from jax import lax
from jax.experimental import pallas as pl
from jax.experimental.pallas import tpu as pltpu

F32 = jnp.float32
BF16 = jnp.bfloat16

D_MODEL = 2048
DEPTH = 4
GLA_HEADS = 4
GLA_DK = 256
GLA_DV = 512
GLA_HK = GLA_HEADS * GLA_DK
GLA_HV = GLA_HEADS * GLA_DV
GLA_RANK = 16
GLA_TAU = 16.0
GLA_BLOCK = 64
GM_WIDTH = 2048
GM_GROUPS = 8
GM_GROUP_DIM = GM_WIDTH // GM_GROUPS
GM_CHUNK = 128
D_FF = 5632
CONV_W = 3
PE_DIM = 256
EPS = 1e-6

V7X_VMEM_BYTES = 64 * 1024 * 1024
V7X_LANES = 128
V7X_SUBLANES = 8

ARB2 = ("arbitrary", "arbitrary")
ARB1 = ("arbitrary",)


def _params(sem, vmem_bytes):
    assert vmem_bytes <= V7X_VMEM_BYTES - (2 << 20), vmem_bytes
    return pltpu.CompilerParams(dimension_semantics=sem, vmem_limit_bytes=int(vmem_bytes))


def _nbytes(shape, dtype):
    n = 1
    for s in shape:
        n *= s
    return n * jnp.dtype(dtype).itemsize


def _rms(x, g):
    return x * lax.rsqrt(jnp.mean(x * x, axis=-1, keepdims=True) + EPS) * g


def _log_sigmoid(x):
    return jnp.minimum(x, 0.0) - jnp.log(1.0 + jnp.exp(-jnp.abs(x)))


def _dot(a, b):
    return jnp.dot(a, b, preferred_element_type=F32)


def _norm0_kernel(x_ref, g_ref, o_ref):
    o_ref[...] = _rms(x_ref[...], g_ref[...]).astype(BF16)


def _norm0(x, g, layer):
    m = x.shape[0]
    tm = 512
    row = pl.BlockSpec((tm, D_MODEL), lambda i: (i, 0))
    vmem = 2 * (_nbytes((tm, D_MODEL), F32) + _nbytes((tm, D_MODEL), BF16)) + 3 * _nbytes((tm, D_MODEL), F32)
    return pl.pallas_call(
        _norm0_kernel,
        grid=(m // tm,),
        in_specs=[row, pl.BlockSpec((None, 1, D_MODEL), lambda i: (layer, 0, 0))],
        out_specs=row,
        out_shape=jax.ShapeDtypeStruct((m, D_MODEL), BF16),
        compiler_params=_params(ARB1, vmem),
        name="norm0",
    )(x, g)


PROJ_CHUNK_ROWS = 128
PROJ_TN = 1024


def _proj_kernel(*refs, has_bias, act, w_is_nk, has_cast):
    refs = list(refs)
    wb_ref = refs.pop()
    x_ref, w_ref = refs[:2]
    b_ref = refs[2] if has_bias else None
    n_in = 2 + has_bias + has_cast
    o_ref = refs[n_in]
    if has_cast:
        refs[n_in + 1][...] = refs[n_in - 1][...].astype(BF16)

    @pl.when(pl.program_id(1) == 0)
    def _():
        w = w_ref[...]
        wb_ref[...] = (w.T if w_is_nk else w).astype(BF16)

    chunks = [slice(r0, r0 + PROJ_CHUNK_ROWS) for r0 in range(0, x_ref.shape[0], PROJ_CHUNK_ROWS)]
    ys = [_dot(x_ref[rows, :], wb_ref[...]) for rows in chunks]
    for rows, y in zip(chunks, ys):
        if has_bias:
            y = y + b_ref[...]
        if act is not None:
            y = act(y)
        o_ref[rows, :] = y.astype(o_ref.dtype)


def _proj(xn, w, layer, out_dtype, nj, col_tile=lambda j: j, bias=None, act=None, w_is_nk=False,
          cast=None, name="proj"):
    m = xn.shape[0]
    tm, tn = 1024, PROJ_TN
    kdim = xn.shape[1]
    n_tiles = m // tm
    if w_is_nk:
        w_block, w_map = (None, tn, kdim), (lambda j, i: (layer, col_tile(j), 0))
    else:
        w_block, w_map = (None, kdim, tn), (lambda j, i: (layer, 0, col_tile(j)))
    in_specs = [pl.BlockSpec((tm, kdim), lambda j, i: (i, 0)), pl.BlockSpec(w_block, w_map)]
    args = [xn, w]
    if bias is not None:
        in_specs.append(pl.BlockSpec((None, 1, tn), lambda j, i: (layer, 0, j)))
        args.append(bias)
    out_specs = [pl.BlockSpec((tm, tn), lambda j, i: (i, j))]
    out_shape = [jax.ShapeDtypeStruct((m, nj * tn), out_dtype)]
    vmem = (2 * (_nbytes((tm, kdim), BF16) + _nbytes((kdim, tn), F32) + _nbytes((tm, tn), out_dtype))
            + _nbytes((kdim, tn), BF16) + 4 * _nbytes((tm, tn), F32))
    if cast is not None:
        cw, cast_layer = cast
        _, rows, cols = cw.shape
        slab = rows // (nj * n_tiles)
        assert slab * nj * n_tiles == rows and slab % (2 * V7X_SUBLANES) == 0, (rows, nj, n_tiles)
        in_specs.append(pl.BlockSpec((None, slab, cols), lambda j, i: (cast_layer, j * n_tiles + i, 0)))
        args.append(cw)
        out_specs.append(pl.BlockSpec((slab, cols), lambda j, i: (j * n_tiles + i, 0)))
        out_shape.append(jax.ShapeDtypeStruct((rows, cols), BF16))
        vmem += 2 * (_nbytes((slab, cols), F32) + _nbytes((slab, cols), BF16))
    outs = pl.pallas_call(
        functools.partial(_proj_kernel, has_bias=bias is not None, act=act, w_is_nk=w_is_nk,
                          has_cast=cast is not None),
        grid=(nj, n_tiles),
        in_specs=in_specs,
        out_specs=out_specs,
        out_shape=out_shape,
        scratch_shapes=[pltpu.VMEM((kdim, tn), BF16)],
        compiler_params=_params(ARB2, vmem),
        name=name,
    )(*args)
    return outs if cast is not None else outs[0]


def _gla_gate_kernel(x_ref, wgd_ref, wgate_ref, bgate_ref, o_ref):
    gl = _dot(x_ref[...], wgd_ref[...])
    z = _dot(gl.astype(BF16), wgate_ref[...]) + bgate_ref[...]
    o_ref[...] = _log_sigmoid(z) * (1.0 / GLA_TAU)


def _gla_gate(xn, w_gd, w_gate, b_gate, layer):
    m = xn.shape[0]
    tm = 512
    vmem = (2 * (_nbytes((tm, D_MODEL), BF16) + _nbytes((tm, GLA_HK), F32)
                 + _nbytes((D_MODEL, V7X_LANES), BF16) + _nbytes((V7X_LANES, GLA_HK), BF16))
            + 4 * _nbytes((tm, GLA_HK), F32))
    return pl.pallas_call(
        _gla_gate_kernel,
        grid=(m // tm,),
        in_specs=[
            pl.BlockSpec((tm, D_MODEL), lambda i: (i, 0)),
            pl.BlockSpec((None, D_MODEL, V7X_LANES), lambda i: (layer, 0, 0)),
            pl.BlockSpec((None, V7X_LANES, GLA_HK), lambda i: (layer, 0, 0)),
            pl.BlockSpec((None, 1, GLA_HK), lambda i: (layer, 0, 0)),
        ],
        out_specs=pl.BlockSpec((tm, GLA_HK), lambda i: (i, 0)),
        out_shape=jax.ShapeDtypeStruct((m, GLA_HK), F32),
        compiler_params=_params(ARB1, vmem),
        name="gla_gate",
    )(xn, w_gd, w_gate, b_gate)


def _gla_rec_kernel(q_ref, k_ref, v_ref, r_ref, la_ref, s0_ref, gain_ref, *rest, n_blocks):
    o_ref, s_ref = rest[-2:]
    blk = GLA_BLOCK

    @pl.when(pl.program_id(1) == 0)
    def _():
        s_ref[...] = s0_ref[...]

    row = lax.broadcasted_iota(jnp.int32, (blk, blk), 0)
    col = lax.broadcasted_iota(jnp.int32, (blk, blk), 1)
    causal = row >= col
    tri = causal.astype(F32).astype(BF16)
    out_g = gain_ref[...]

    heads = range(GLA_HEADS)
    ksl = [slice(hh * GLA_DK, (hh + 1) * GLA_DK) for hh in heads]
    vsl = [slice(hh * GLA_DV, (hh + 1) * GLA_DV) for hh in heads]
    rsl = [slice(n * blk, (n + 1) * blk) for n in range(n_blocks)]
    q_dec, k_inv, k_end, decay = [], [], [], []
    for n in range(n_blocks):
        g = la_ref[rsl[n], :]
        g1 = g.astype(BF16)
        e1 = g - g1.astype(F32)
        g2 = e1.astype(BF16)
        g3 = (e1 - g2.astype(F32)).astype(BF16)
        cum = _dot(tri, g1) + _dot(tri, g2) + _dot(tri, g3)
        last = cum[blk - 1:blk, :]
        q = q_ref[rsl[n], :] * (GLA_DK ** -0.5)
        k = k_ref[rsl[n], :]
        q_dec.append((q * jnp.exp(cum)).astype(BF16))
        k_inv.append((k * jnp.exp(-cum)).astype(BF16))
        k_end.append((k * jnp.exp(last - cum)).astype(BF16))
        decay.append(jnp.exp(last))
    o_intra, kv = {}, {}
    for n in range(n_blocks):
        for hh in heads:
            att = lax.dot_general(q_dec[n][:, ksl[hh]], k_inv[n][:, ksl[hh]], (((1,), (1,)), ((), ())),
                                  preferred_element_type=F32)
            att = jnp.where(causal, att, 0.0)
            vb = v_ref[rsl[n], vsl[hh]]
            o_intra[n, hh] = _dot(att.astype(BF16), vb)
            kv[n, hh] = lax.dot_general(k_end[n][:, ksl[hh]], vb, (((0,), (0,)), ((), ())),
                                        preferred_element_type=F32)
    s = [s_ref[0, hh] for hh in heads]
    for n in range(n_blocks):
        for hh in heads:
            o = o_intra[n, hh] + _dot(q_dec[n][:, ksl[hh]], s[hh].astype(BF16))
            dcol = jnp.broadcast_to(decay[n][:, ksl[hh]], (V7X_LANES, GLA_DK)).T
            s[hh] = s[hh] * jnp.tile(dcol, (1, GLA_DV // V7X_LANES)) + kv[n, hh]
            on = _rms(o, out_g)
            r = r_ref[rsl[n], vsl[hh]]
            o_ref[rsl[n], vsl[hh]] = (on * (r * jax.nn.sigmoid(r))).astype(BF16)
    for hh in heads:
        s_ref[0, hh] = s[hh]


def _gla_rec(qkr, v, loga, s0, out_g, layer, n_seq, seq_len, s_all=None):
    m = qkr.shape[0]
    tc = min(seq_len, 256)
    nt = seq_len // tc
    row_map = lambda c: (lambda b, t: (b * nt + t, c))
    s0_spec = pl.BlockSpec((None, 1, GLA_HEADS, GLA_DK, GLA_DV), lambda b, t: (layer, b, 0, 0, 0))
    st_spec = s0_spec
    in_specs = [
        pl.BlockSpec((tc, GLA_HK), row_map(0)),
        pl.BlockSpec((tc, GLA_HK), row_map(1)),
        pl.BlockSpec((tc, GLA_HV), row_map(0)),
        pl.BlockSpec((tc, GLA_HV), row_map(1)),
        pl.BlockSpec((tc, GLA_HK), row_map(0)),
        s0_spec,
        pl.BlockSpec((None, 1, GLA_DV), lambda b, t: (layer, 0, 0)),
    ]
    args = [qkr, qkr, v, qkr, loga, s0, out_g]
    aliases = {}
    if s_all is not None:
        aliases = {len(args): 1}
        in_specs.append(pl.BlockSpec(memory_space=pl.ANY))
        args.append(s_all)
    vmem = (2 * (3 * _nbytes((tc, GLA_HK), F32) + _nbytes((tc, GLA_HV), F32)
                 + 2 * _nbytes((tc, GLA_HV), BF16)
                 + 2 * _nbytes((GLA_HEADS, GLA_DK, GLA_DV), F32))
            + (tc // GLA_BLOCK) * (8 * _nbytes((GLA_BLOCK, GLA_HK), F32) + _nbytes((GLA_BLOCK, GLA_HV), F32)
                                   + _nbytes((GLA_HEADS, GLA_DK, GLA_DV), F32))
            + 8 * _nbytes((GLA_DK, GLA_DV), F32))
    return pl.pallas_call(
        functools.partial(_gla_rec_kernel, n_blocks=tc // GLA_BLOCK),
        grid=(n_seq, nt),
        in_specs=in_specs,
        out_specs=[pl.BlockSpec((tc, GLA_HV), row_map(0)), st_spec],
        out_shape=[jax.ShapeDtypeStruct((m, GLA_HV), BF16),
                   jax.ShapeDtypeStruct((s0.shape[0], n_seq, GLA_HEADS, GLA_DK, GLA_DV), F32)],
        input_output_aliases=aliases,
        compiler_params=_params(ARB2, vmem),
        name="gla_rec",
    )(*args)


def _out_proj_kernel(x_ref, w_ref, res_ref, g_ref, h_ref, xn_ref):
    h = res_ref[...] + _dot(x_ref[...], w_ref[...])
    h_ref[...] = h
    xn_ref[...] = _rms(h, g_ref[...]).astype(BF16)


def _out_proj(x, w, res, g_ffn, layer_w, layer_g):
    m = x.shape[0]
    tm = 512
    vmem = (2 * (2 * _nbytes((tm, D_MODEL), BF16) + 2 * _nbytes((tm, D_MODEL), F32)
                 + _nbytes((D_MODEL, D_MODEL), BF16))
            + 3 * _nbytes((tm, D_MODEL), F32))
    row = pl.BlockSpec((tm, D_MODEL), lambda i: (i, 0))
    return pl.pallas_call(
        _out_proj_kernel,
        grid=(m // tm,),
        in_specs=[
            row,
            pl.BlockSpec((None, D_MODEL, D_MODEL), lambda i: (layer_w, 0, 0)),
            row,
            pl.BlockSpec((None, 1, D_MODEL), lambda i: (layer_g, 0, 0)),
        ],
        out_specs=[row, row],
        out_shape=[jax.ShapeDtypeStruct((m, D_MODEL), F32),
                   jax.ShapeDtypeStruct((m, D_MODEL), BF16)],
        compiler_params=_params(ARB1, vmem),
        name="out_proj",
    )(x, w, res, g_ffn)


def _gm_out_kernel(z_ref, lng_ref, lnb_ref, ws_ref, bs_ref, wo_ref, res_ref, g_ref,
                   h_ref, xn_ref, *rest, chunk, n_chunks, emit_v):
    if emit_v:
        vn_ref, gate_ref = rest
    else:
        (gate_ref,) = rest
    v = z_ref[:, GM_WIDTH:]
    mu = jnp.mean(v, axis=-1, keepdims=True)
    vc = v - mu
    var = jnp.mean(vc * vc, axis=-1, keepdims=True)
    vn = vc * lax.rsqrt(var + EPS) * lng_ref[...] + lnb_ref[...]
    if emit_v:
        vn_ref[...] = vn
    vb = vn.astype(BF16)

    row = lax.broadcasted_iota(jnp.int32, (chunk, chunk), 0)
    col = lax.broadcasted_iota(jnp.int32, (chunk, chunk), 1)
    tril = (row >= col).astype(F32)
    for gi in range(GM_GROUPS):
        cs = slice(gi * GM_GROUP_DIM, (gi + 1) * GM_GROUP_DIM)
        wsg = (ws_ref[gi] * tril).astype(BF16)
        bsg = bs_ref[:, gi:gi + 1]
        for c in range(n_chunks):
            rows = slice(c * chunk, (c + 1) * chunk)
            mixed = _dot(wsg, vb[rows, cs]) + bsg
            gate_ref[rows, cs] = (z_ref[rows, cs] * mixed).astype(BF16)

    h = res_ref[...] + _dot(gate_ref[...], wo_ref[...])
    h_ref[...] = h
    xn_ref[...] = _rms(h, g_ref[...]).astype(BF16)


def _gm_out(z, ln_g, ln_b, ws, bs_t, wo, res, g_ffn, layer_m, layer_g, chunk, emit_v):
    m = z.shape[0]
    tm = 256
    row = pl.BlockSpec((tm, D_MODEL), lambda i: (i, 0))
    vec = pl.BlockSpec((None, 1, D_MODEL), lambda i: (layer_m, 0, 0))
    vmem = (2 * (_nbytes((tm, 2 * GM_WIDTH), F32) + 3 * _nbytes((tm, D_MODEL), F32)
                 + _nbytes((tm, D_MODEL), BF16) + _nbytes((D_MODEL, D_MODEL), BF16)
                 + _nbytes((GM_GROUPS, chunk, chunk), F32))
            + 2 * _nbytes((tm, D_MODEL), BF16) + 4 * _nbytes((tm, D_MODEL), F32))
    out_specs = [row, row]
    out_shape = [jax.ShapeDtypeStruct((m, D_MODEL), F32), jax.ShapeDtypeStruct((m, D_MODEL), BF16)]
    if emit_v:
        out_specs.append(row)
        out_shape.append(jax.ShapeDtypeStruct((m, GM_WIDTH), F32))
    return pl.pallas_call(
        functools.partial(_gm_out_kernel, chunk=chunk, n_chunks=tm // chunk, emit_v=emit_v),
        grid=(m // tm,),
        in_specs=[
            pl.BlockSpec((tm, 2 * GM_WIDTH), lambda i: (i, 0)),
            vec,
            vec,
            pl.BlockSpec((GM_GROUPS, chunk, chunk), lambda i: (0, 0, 0)),
            pl.BlockSpec((chunk, GM_GROUPS), lambda i: (0, 0)),
            pl.BlockSpec((None, D_MODEL, D_MODEL), lambda i: (layer_m, 0, 0)),
            row,
            pl.BlockSpec((None, 1, D_MODEL), lambda i: (layer_g, 0, 0)),
        ],
        out_specs=out_specs,
        out_shape=out_shape,
        scratch_shapes=[pltpu.VMEM((tm, GM_WIDTH), BF16)],
        compiler_params=_params(ARB1, vmem),
        name="gm_out",
    )(z, ln_g, ln_b, ws, bs_t, wo, res, g_ffn)


PAST_ROWS = V7X_SUBLANES
FFN_CHUNK_ROWS = 512
FFN_TF = 512


def _ffn_kernel(x_ref, wg_ref, wv_ref, cwg_ref, cwv_ref, cbg_ref, cbv_ref, wd_ref,
                pg_ref, pv_ref, f_ref, csg_ref, csv_ref,
                workg_ref, workv_ref, act_ref, carg_ref, carv_ref,
                *, n_seq, seq_rows, tiles_per_seq, chunk_rows):
    i = pl.program_id(0)
    j = pl.program_id(1)
    stride = seq_rows + PAST_ROWS
    tm = n_seq * seq_rows
    chunks = [(r0, chunk_rows) for r0 in range(0, tm, chunk_rows)]

    def pieces(r0, rows):
        step = min(rows, seq_rows)
        return [(t, step, t + PAST_ROWS * (t // seq_rows + 1)) for t in range(r0, r0 + rows, step)]

    @pl.when(j == 0)
    def _():
        f_ref[...] = jnp.zeros_like(f_ref)

    if tiles_per_seq > 1:
        @pl.when((i == 0) & (j == 0))
        def _():
            carg_ref[...] = jnp.zeros_like(carg_ref)
            carv_ref[...] = jnp.zeros_like(carv_ref)

    halves = ((wg_ref, pg_ref, carg_ref, csg_ref, workg_ref), (wv_ref, pv_ref, carv_ref, csv_ref, workv_ref))
    for w_ref, past_ref, car_ref, cs_ref, work_ref in halves:
        if tiles_per_seq > 1:
            first = (i % tiles_per_seq) == 0
            work_ref[0:PAST_ROWS, :] = jnp.where(first, past_ref[0], car_ref[j])
        else:
            for s in range(n_seq):
                work_ref[s * stride:s * stride + PAST_ROWS, :] = past_ref[s]
    for r0, rows in chunks:
        for w_ref, past_ref, car_ref, cs_ref, work_ref in halves:
            a = _dot(x_ref[r0:r0 + rows, :], w_ref[...])
            for t, n, wrow in pieces(r0, rows):
                work_ref[wrow:wrow + n, :] = a[t - r0:t - r0 + n, :]
    for w_ref, past_ref, car_ref, cs_ref, work_ref in halves:
        for s in range(n_seq):
            cs_ref[s] = work_ref[s * stride + seq_rows:(s + 1) * stride, :]
        if tiles_per_seq > 1:
            car_ref[j] = work_ref[seq_rows:stride, :]

    def conv(work_ref, cw_ref, cb_ref, wrow, n):
        x0 = work_ref[wrow:wrow + n, :]
        x1 = work_ref[wrow - 1:wrow - 1 + n, :]
        x2 = work_ref[wrow - 2:wrow - 2 + n, :]
        return cb_ref[...] + x2 * cw_ref[0:1, :] + x1 * cw_ref[1:2, :] + x0 * cw_ref[2:3, :]

    wd = wd_ref[...].astype(BF16)
    for r0, rows in chunks:
        for t, n, wrow in pieces(r0, rows):
            cg = conv(workg_ref, cwg_ref, cbg_ref, wrow, n)
            cv = conv(workv_ref, cwv_ref, cbv_ref, wrow, n)
            act_ref[t:t + n, :] = (cg * jax.nn.sigmoid(cg) * cv).astype(BF16)
        f_ref[r0:r0 + rows, :] += _dot(act_ref[r0:r0 + rows, :], wd)


def _ffn(xn, w_up, conv_w, conv_b, w_down, past, layer, seq_len):
    m = xn.shape[0]
    tm, tf = 1024, FFN_TF
    nj = D_FF // tf
    if seq_len >= tm:
        n_seq, seq_rows, tiles_per_seq = 1, tm, seq_len // tm
    else:
        n_seq, seq_rows, tiles_per_seq = tm // seq_len, seq_len, 1
    n_tiles = m // tm
    past_map = lambda off: (lambda i, j: (i // tiles_per_seq, 0, j + off))
    cs_spec = pl.BlockSpec((n_seq, PAST_ROWS, tf), lambda i, j: (i, 0, j))
    cs_shape = jax.ShapeDtypeStruct((n_tiles * n_seq, PAST_ROWS, D_FF), F32)
    work = pltpu.VMEM((n_seq * (seq_rows + PAST_ROWS), tf), F32)
    car = pltpu.VMEM((nj, PAST_ROWS, tf), F32)
    vmem = (2 * (_nbytes((tm, D_MODEL), BF16) + 2 * _nbytes((D_MODEL, tf), BF16) + _nbytes((tf, D_MODEL), F32)
                 + _nbytes((tm, D_MODEL), F32) + 4 * _nbytes((n_seq, PAST_ROWS, tf), F32))
            + _nbytes((tf, D_MODEL), BF16)
            + 2 * _nbytes((tm + n_seq * PAST_ROWS, tf), F32) + _nbytes((tm, tf), BF16)
            + 2 * _nbytes((nj, PAST_ROWS, tf), F32)
            + 4 * _nbytes((tm, tf), F32))
    f, csg, csv = pl.pallas_call(
        functools.partial(_ffn_kernel, n_seq=n_seq, seq_rows=seq_rows, tiles_per_seq=tiles_per_seq,
                          chunk_rows=FFN_CHUNK_ROWS),
        grid=(n_tiles, nj),
        in_specs=[
            pl.BlockSpec((tm, D_MODEL), lambda i, j: (i, 0)),
            pl.BlockSpec((D_MODEL, tf), lambda i, j: (0, j)),
            pl.BlockSpec((D_MODEL, tf), lambda i, j: (0, j + nj)),
            pl.BlockSpec((None, CONV_W, tf), lambda i, j: (layer, 0, j)),
            pl.BlockSpec((None, CONV_W, tf), lambda i, j: (layer, 0, j + nj)),
            pl.BlockSpec((None, 1, tf), lambda i, j: (layer, 0, j)),
            pl.BlockSpec((None, 1, tf), lambda i, j: (layer, 0, j + nj)),
            pl.BlockSpec((None, tf, D_MODEL), lambda i, j: (layer, j, 0)),
            pl.BlockSpec((n_seq, PAST_ROWS, tf), past_map(0)),
            pl.BlockSpec((n_seq, PAST_ROWS, tf), past_map(nj)),
        ],
        out_specs=[pl.BlockSpec((tm, D_MODEL), lambda i, j: (i, 0)), cs_spec, cs_spec],
        out_shape=[jax.ShapeDtypeStruct((m, D_MODEL), F32), cs_shape, cs_shape],
        scratch_shapes=[work, work, pltpu.VMEM((tm, tf), BF16), car, car],
        compiler_params=_params(ARB2, vmem),
        name="ffn",
    )(xn, w_up, w_up, conv_w, conv_w, conv_b, conv_b, w_down, past, past)
    keep = slice(tiles_per_seq - 1, None, tiles_per_seq)
    cstate = jnp.concatenate([csg[keep], csv[keep]], axis=-1)[:, PAST_ROWS - (CONV_W - 1):, :]
    return f, cstate


def _pe_kernel(h1_ref, f_ref, p_ref, g_ref, wg_ref, wp_ref, gnext_ref, *out_refs, final):
    h2 = h1_ref[...] + f_ref[...]
    hn = _rms(h2, g_ref[...]).astype(BF16)
    gate = jax.nn.sigmoid(_dot(hn, wg_ref[...]))
    proj = _dot(p_ref[...].astype(BF16), wp_ref[...])
    h3 = h2 + gate * proj
    if final:
        (y_ref,) = out_refs
        y_ref[...] = _rms(h3, gnext_ref[...])
    else:
        h_ref, xn_ref = out_refs
        h_ref[...] = h3
        xn_ref[...] = _rms(h3, gnext_ref[...]).astype(BF16)


def _pe(h1, f, p, g_pe, w_gate, w_proj, g_next, layer, layer_next):
    m = h1.shape[0]
    tm = 256
    final = layer_next is None
    row = pl.BlockSpec((tm, D_MODEL), lambda i: (i, 0))
    vmem = (2 * (4 * _nbytes((tm, D_MODEL), F32) + _nbytes((tm, PE_DIM), F32)
                 + _nbytes((D_MODEL, D_MODEL), BF16) + _nbytes((PE_DIM, D_MODEL), BF16))
            + 6 * _nbytes((tm, D_MODEL), F32))
    if final:
        gnext_spec = pl.BlockSpec((1, D_MODEL), lambda i: (0, 0))
        out_specs = [row]
        out_shape = [jax.ShapeDtypeStruct((m, D_MODEL), F32)]
    else:
        gnext_spec = pl.BlockSpec((None, 1, D_MODEL), lambda i: (layer_next, 0, 0))
        out_specs = [row, row]
        out_shape = [jax.ShapeDtypeStruct((m, D_MODEL), F32), jax.ShapeDtypeStruct((m, D_MODEL), BF16)]
    return pl.pallas_call(
        functools.partial(_pe_kernel, final=final),
        grid=(m // tm,),
        in_specs=[
            row,
            row,
            pl.BlockSpec((None, tm, PE_DIM), lambda i: (layer, i, 0)),
            pl.BlockSpec((None, 1, D_MODEL), lambda i: (layer, 0, 0)),
            pl.BlockSpec((None, D_MODEL, D_MODEL), lambda i: (layer, 0, 0)),
            pl.BlockSpec((None, PE_DIM, D_MODEL), lambda i: (layer, 0, 0)),
            gnext_spec,
        ],
        out_specs=out_specs,
        out_shape=out_shape,
        compiler_params=_params(ARB1, vmem),
        name="pe",
    )(h1, f, p, g_pe, w_gate, w_proj, g_next)


def _trunk(x, p, gla_s0, conv_past, w, n_seq, seq_len, emit_v, w_up_bf16=None):
    m = n_seq * seq_len
    h = x.reshape(m, D_MODEL)
    p = p.reshape(DEPTH, m, PE_DIM)
    chunk = min(seq_len, GM_CHUNK)
    gla_states, conv_states, gm_rows = None, [], []
    make_w_up = w_up_bf16 is None
    if make_w_up:
        w_up_bf16 = []
    xn = _norm0(h, w["norm_mix"], 0)
    for i in range(DEPTH):
        jm = i // 2
        cast = (w["ffn_w_up"], i) if make_w_up else None
        if i % 2 == 0:
            qkr = _proj(xn, w["gla_w_in_t"], jm, F32, 4, col_tile=lambda j: j + 2 * (j // 2), w_is_nk=True,
                        cast=cast, name="gla_qkr")
            if make_w_up:
                qkr, w_up_i = qkr
                w_up_bf16.append(w_up_i)
            v = _proj(xn, w["gla_w_in_t"], jm, BF16, 2, col_tile=lambda j: j + 2, w_is_nk=True, name="gla_v")
            loga = _gla_gate(xn, w["gla_w_gd"], w["gla_w_gate"], w["gla_b_gate"], jm)
            og, gla_states = _gla_rec(qkr, v, loga, gla_s0, w["gla_out_norm"], jm, n_seq, seq_len,
                                      s_all=gla_states)
            h1, xnf = _out_proj(og, w["gla_w_o"], h, w["norm_ffn"], jm, i)
        else:
            z = _proj(xn, w["gm_w_in"], jm, F32, 2 * GM_WIDTH // PROJ_TN, bias=w["gm_b_in"],
                      act=jax.nn.gelu, cast=cast, name="gm_in")
            if make_w_up:
                z, w_up_i = z
                w_up_bf16.append(w_up_i)
            ws =w["gm_w_s"][jm][:, :chunk, :chunk]
            bs_t = w["gm_b_s"][jm][:, :chunk].T
            outs = _gm_out(z, w["gm_ln_g"], w["gm_ln_b"], ws, bs_t, w["gm_w_o"], h,
                           w["norm_ffn"], jm, i, chunk, emit_v)
            h1, xnf = outs[0], outs[1]
            if emit_v:
                gm_rows.append(outs[2].reshape(n_seq, seq_len, GM_WIDTH))
        f, cstate = _ffn(xnf, w_up_bf16[i], w["ffn_conv_w"], w["ffn_conv_b"], w["ffn_w_down"],
                         conv_past[i], i, seq_len)
        conv_states.append(cstate)
        if i + 1 < DEPTH:
            h, xn = _pe(h1, f, p, w["norm_pe"], w["pe_w_gate"], w["pe_w_proj"], w["norm_mix"], i, i + 1)
        else:
            (y,) = _pe(h1, f, p, w["norm_pe"], w["pe_w_gate"], w["pe_w_proj"], w["norm_final"], i, None)
    y = y.reshape(n_seq, seq_len, D_MODEL)
    gm_v = jnp.stack(gm_rows) if emit_v else None
    return y, gla_states, jnp.stack(conv_states), gm_v, w_up_bf16


def _pad_past(past):
    return jnp.pad(past, ((0, 0), (0, 0), (PAST_ROWS - (CONV_W - 1), 0), (0, 0)))


def kernel(x_prompt, x_sample, p_prompt, p_sample, state_gla, state_ffn_conv, norm_mix, norm_ffn, norm_pe, norm_final, gla_w_in, gla_w_gate, gla_b_gate, gla_out_norm, gla_w_o, gm_w_in, gm_b_in, gm_ln_g, gm_ln_b, gm_w_s, gm_b_s, gm_w_o, ffn_w_up, ffn_conv_w, ffn_conv_b, ffn_w_down, pe_w_proj, pe_w_gate):
    r_end = 2 * GLA_HK + 2 * GLA_HV
    w = {
        "norm_mix": norm_mix[:, None, :],
        "norm_ffn": norm_ffn[:, None, :],
        "norm_pe": norm_pe[:, None, :],
        "norm_final": norm_final[None, :],
        "gla_w_in_t": jnp.swapaxes(gla_w_in, 1, 2),
        "gla_w_gd": jnp.pad(gla_w_in[:, :, r_end:], ((0, 0), (0, 0), (0, V7X_LANES - GLA_RANK))).astype(BF16),
        "gla_w_gate": jnp.pad(gla_w_gate, ((0, 0), (0, V7X_LANES - GLA_RANK), (0, 0))).astype(BF16),
        "gla_b_gate": gla_b_gate[:, None, :],
        "gla_out_norm": gla_out_norm[:, None, :],
        "gla_w_o": gla_w_o.astype(BF16),
        "gm_w_in": gm_w_in,
        "gm_b_in": gm_b_in[:, None, :],
        "gm_ln_g": gm_ln_g[:, None, :],
        "gm_ln_b": gm_ln_b[:, None, :],
        "gm_w_s": gm_w_s,
        "gm_b_s": gm_b_s,
        "gm_w_o": gm_w_o.astype(BF16),
        "ffn_w_up": ffn_w_up,
        "ffn_conv_w": ffn_conv_w,
        "ffn_conv_b": ffn_conv_b[:, None, :],
        "ffn_w_down": ffn_w_down,
        "pe_w_proj": pe_w_proj.astype(BF16),
        "pe_w_gate": pe_w_gate.astype(BF16),
    }
    b_prompt, seq = x_prompt.shape[0], x_prompt.shape[1]
    b_sample, dec_seq = x_sample.shape[0], x_sample.shape[1]
    n_gla = state_gla.shape[0]
    gla_s0_prompt = jnp.zeros((n_gla, b_prompt, GLA_HEADS, GLA_DK, GLA_DV), F32)
    conv_s0_prompt = jnp.zeros((DEPTH, b_prompt, PAST_ROWS, 2 * D_FF), F32)
    y_prompt, gla_state_prompt, conv_state_prompt, _, w_up_bf16 = _trunk(
        x_prompt, p_prompt, gla_s0_prompt, conv_s0_prompt, w, b_prompt, seq, False)
    y_sample, gla_state_sample, conv_state_sample, gm_v_sample, _ = _trunk(
        x_sample, p_sample, state_gla, _pad_past(state_ffn_conv), w, b_sample, dec_seq, True, w_up_bf16)
    return (y_prompt, y_sample, gla_state_prompt, gla_state_sample,
            conv_state_prompt, conv_state_sample, gm_v_sample)
```

```python
import functools

import jax
import jax.numpy as jnp
from jax import lax
from jax.experimental import pallas as pl
from jax.experimental.pallas import tpu as pltpu

F32 = jnp.float32
BF16 = jnp.bfloat16

D_MODEL = 2048
DEPTH = 4
GLA_HEADS = 4
GLA_DK = 256
GLA_DV = 512
GLA_HK = GLA_HEADS * GLA_DK
GLA_HV = GLA_HEADS * GLA_DV
GLA_RANK = 16
GLA_TAU = 16.0
GLA_BLOCK = 64
GM_WIDTH = 2048
GM_GROUPS = 8
GM_GROUP_DIM = GM_WIDTH // GM_GROUPS
GM_CHUNK = 128
D_FF = 5632
CONV_W = 3
PE_DIM = 256
EPS = 1e-6

V7X_VMEM_BYTES = 64 * 1024 * 1024
V7X_LANES = 128
V7X_SUBLANES = 8

STREAM_TM = 1024
ROW_TM = 512
WIDE_TM = 256
GLA_STEP_ROWS = 256

ARB2 = ("arbitrary", "arbitrary")
ARB1 = ("arbitrary",)


def _params(sem, vmem_bytes):
    assert vmem_bytes <= V7X_VMEM_BYTES - (2 << 20), vmem_bytes
    return pltpu.CompilerParams(dimension_semantics=sem, vmem_limit_bytes=int(vmem_bytes))


def _nbytes(shape, dtype):
    n = 1
    for s in shape:
        n *= s
    return n * jnp.dtype(dtype).itemsize


def _rms(x, g):
    return x * lax.rsqrt(jnp.mean(x * x, axis=-1, keepdims=True) + EPS) * g


def _log_sigmoid(x):
    return jnp.minimum(x, 0.0) - jnp.log(1.0 + jnp.exp(-jnp.abs(x)))


def _dot(a, b):
    return jnp.dot(a, b, preferred_element_type=F32)


def _norm0_kernel(x_ref, g_ref, o_ref):
    o_ref[...] = _rms(x_ref[...], g_ref[...]).astype(BF16)


def _norm0(x, g):
    m = x.shape[0]
    tm = ROW_TM
    row = pl.BlockSpec((tm, D_MODEL), lambda i: (i, 0))
    vmem = 2 * (_nbytes((tm, D_MODEL), F32) + _nbytes((tm, D_MODEL), BF16)) + 3 * _nbytes((tm, D_MODEL), F32)
    return pl.pallas_call(
        _norm0_kernel,
        grid=(m // tm,),
        in_specs=[row, pl.BlockSpec((None, 1, D_MODEL), lambda i: (0, 0, 0))],
        out_specs=row,
        out_shape=jax.ShapeDtypeStruct((m, D_MODEL), BF16),
        compiler_params=_params(ARB1, vmem),
        name="norm0",
    )(x, g)


PROJ_CHUNK_ROWS = 128
PROJ_TN = 1024


def _proj_kernel(*refs, has_bias, act, w_is_nk, has_cast):
    refs = list(refs)
    wb_ref = refs.pop()
    x_ref, w_ref = refs[:2]
    b_ref = refs[2] if has_bias else None
    n_in = 2 + has_bias + has_cast
    o_ref = refs[n_in]
    if has_cast:
        refs[n_in + 1][...] = refs[n_in - 1][...].astype(BF16)

    @pl.when(pl.program_id(1) == 0)
    def _():
        w = w_ref[...]
        wb_ref[...] = (w.T if w_is_nk else w).astype(BF16)

    chunks = [slice(r0, r0 + PROJ_CHUNK_ROWS) for r0 in range(0, x_ref.shape[0], PROJ_CHUNK_ROWS)]
    ys = [_dot(x_ref[rows, :], wb_ref[...]) for rows in chunks]
    for rows, y in zip(chunks, ys):
        if has_bias:
            y = y + b_ref[...]
        if act is not None:
            y = act(y)
        o_ref[rows, :] = y.astype(o_ref.dtype)


def _proj(xn, w, layer, out_dtype, nj, col_tile=lambda j: j, bias=None, act=None, w_is_nk=False,
          cast=None, name="proj"):
    m = xn.shape[0]
    tm, tn = STREAM_TM, PROJ_TN
    kdim = xn.shape[1]
    n_tiles = m // tm
    if w_is_nk:
        w_block, w_map = (None, tn, kdim), (lambda j, i: (layer, col_tile(j), 0))
    else:
        w_block, w_map = (None, kdim, tn), (lambda j, i: (layer, 0, col_tile(j)))
    in_specs = [pl.BlockSpec((tm, kdim), lambda j, i: (i, 0)), pl.BlockSpec(w_block, w_map)]
    args = [xn, w]
    if bias is not None:
        in_specs.append(pl.BlockSpec((None, 1, tn), lambda j, i: (layer, 0, j)))
        args.append(bias)
    out_specs = [pl.BlockSpec((tm, tn), lambda j, i: (i, j))]
    out_shape = [jax.ShapeDtypeStruct((m, nj * tn), out_dtype)]
    vmem = (2 * (_nbytes((tm, kdim), BF16) + _nbytes((kdim, tn), F32) + _nbytes((tm, tn), out_dtype))
            + _nbytes((kdim, tn), BF16) + 4 * _nbytes((tm, tn), F32))
    if cast is not None:
        cw, cast_layer = cast
        _, rows, cols = cw.shape
        slab = rows // (nj * n_tiles)
        assert slab * nj * n_tiles == rows and slab % (2 * V7X_SUBLANES) == 0, (rows, nj, n_tiles)
        in_specs.append(pl.BlockSpec((None, slab, cols), lambda j, i: (cast_layer, j * n_tiles + i, 0)))
        args.append(cw)
        out_specs.append(pl.BlockSpec((slab, cols), lambda j, i: (j * n_tiles + i, 0)))
        out_shape.append(jax.ShapeDtypeStruct((rows, cols), BF16))
        vmem += 2 * (_nbytes((slab, cols), F32) + _nbytes((slab, cols), BF16))
    outs = pl.pallas_call(
        functools.partial(_proj_kernel, has_bias=bias is not None, act=act, w_is_nk=w_is_nk,
                          has_cast=cast is not None),
        grid=(nj, n_tiles),
        in_specs=in_specs,
        out_specs=out_specs,
        out_shape=out_shape,
        scratch_shapes=[pltpu.VMEM((kdim, tn), BF16)],
        compiler_params=_params(ARB2, vmem),
        name=name,
    )(*args)
    return outs if cast is not None else outs[0]


def _gla_gate_kernel(x_ref, wgd_ref, wgate_ref, bgate_ref, o_ref):
    gl = _dot(x_ref[...], wgd_ref[...])
    z = _dot(gl.astype(BF16), wgate_ref[...]) + bgate_ref[...]
    o_ref[...] = _log_sigmoid(z) * (1.0 / GLA_TAU)


def _gla_gate(xn, w_gd, w_gate, b_gate, layer):
    m = xn.shape[0]
    tm = ROW_TM
    vmem = (2 * (_nbytes((tm, D_MODEL), BF16) + _nbytes((tm, GLA_HK), F32)
                 + _nbytes((D_MODEL, V7X_LANES), BF16) + _nbytes((V7X_LANES, GLA_HK), BF16))
            + 4 * _nbytes((tm, GLA_HK), F32))
    return pl.pallas_call(
        _gla_gate_kernel,
        grid=(m // tm,),
        in_specs=[
            pl.BlockSpec((tm, D_MODEL), lambda i: (i, 0)),
            pl.BlockSpec((None, D_MODEL, V7X_LANES), lambda i: (layer, 0, 0)),
            pl.BlockSpec((None, V7X_LANES, GLA_HK), lambda i: (layer, 0, 0)),
            pl.BlockSpec((None, 1, GLA_HK), lambda i: (layer, 0, 0)),
        ],
        out_specs=pl.BlockSpec((tm, GLA_HK), lambda i: (i, 0)),
        out_shape=jax.ShapeDtypeStruct((m, GLA_HK), F32),
        compiler_params=_params(ARB1, vmem),
        name="gla_gate",
    )(xn, w_gd, w_gate, b_gate)


def _gla_rec_kernel(q_ref, k_ref, v_ref, r_ref, la_ref, s0_ref, gain_ref, *rest, n_blocks):
    _, o_ref, s_ref = rest
    blk = GLA_BLOCK

    @pl.when(pl.program_id(1) == 0)
    def _():
        s_ref[...] = s0_ref[...]

    row = lax.broadcasted_iota(jnp.int32, (blk, blk), 0)
    col = lax.broadcasted_iota(jnp.int32, (blk, blk), 1)
    causal = row >= col
    tri = causal.astype(F32).astype(BF16)
    out_g = gain_ref[...]

    heads = range(GLA_HEADS)
    ksl = [slice(hh * GLA_DK, (hh + 1) * GLA_DK) for hh in heads]
    vsl = [slice(hh * GLA_DV, (hh + 1) * GLA_DV) for hh in heads]
    rsl = [slice(n * blk, (n + 1) * blk) for n in range(n_blocks)]
    q_dec, k_inv, k_end, decay = [], [], [], []
    for n in range(n_blocks):
        g = la_ref[rsl[n], :]
        g1 = g.astype(BF16)
        e1 = g - g1.astype(F32)
        g2 = e1.astype(BF16)
        g3 = (e1 - g2.astype(F32)).astype(BF16)
        cum = _dot(tri, g1) + _dot(tri, g2) + _dot(tri, g3)
        last = cum[blk - 1:blk, :]
        q = q_ref[rsl[n], :] * (GLA_DK ** -0.5)
        k = k_ref[rsl[n], :]
        q_dec.append((q * jnp.exp(cum)).astype(BF16))
        k_inv.append((k * jnp.exp(-cum)).astype(BF16))
        k_end.append((k * jnp.exp(last - cum)).astype(BF16))
        decay.append(jnp.exp(last))
    o_intra, kv = {}, {}
    for n in range(n_blocks):
        for hh in heads:
            att = lax.dot_general(q_dec[n][:, ksl[hh]], k_inv[n][:, ksl[hh]], (((1,), (1,)), ((), ())),
                                  preferred_element_type=F32)
            att = jnp.where(causal, att, 0.0)
            vb = v_ref[rsl[n], vsl[hh]]
            o_intra[n, hh] = _dot(att.astype(BF16), vb)
            kv[n, hh] = lax.dot_general(k_end[n][:, ksl[hh]], vb, (((0,), (0,)), ((), ())),
                                        preferred_element_type=F32)
    s = [s_ref[0, hh] for hh in heads]
    for n in range(n_blocks):
        for hh in heads:
            o = o_intra[n, hh] + _dot(q_dec[n][:, ksl[hh]], s[hh].astype(BF16))
            dcol = jnp.broadcast_to(decay[n][:, ksl[hh]], (V7X_LANES, GLA_DK)).T
            s[hh] = s[hh] * jnp.tile(dcol, (1, GLA_DV // V7X_LANES)) + kv[n, hh]
            on = _rms(o, out_g)
            r = r_ref[rsl[n], vsl[hh]]
            o_ref[rsl[n], vsl[hh]] = (on * (r * jax.nn.sigmoid(r))).astype(BF16)
    for hh in heads:
        s_ref[0, hh] = s[hh]


def _gla_rec(qkr, v, loga, s0, out_g, layer, n_seq, seq_len, s_all):
    m = qkr.shape[0]
    tc = min(seq_len, GLA_STEP_ROWS)
    nt = seq_len // tc
    row_map = lambda c: (lambda b, t: (b * nt + t, c))
    s0_spec = pl.BlockSpec((None, 1, GLA_HEADS, GLA_DK, GLA_DV), lambda b, t: (layer, b, 0, 0, 0))
    st_spec = s0_spec
    in_specs = [
        pl.BlockSpec((tc, GLA_HK), row_map(0)),
        pl.BlockSpec((tc, GLA_HK), row_map(1)),
        pl.BlockSpec((tc, GLA_HV), row_map(0)),
        pl.BlockSpec((tc, GLA_HV), row_map(1)),
        pl.BlockSpec((tc, GLA_HK), row_map(0)),
        s0_spec,
        pl.BlockSpec((None, 1, GLA_DV), lambda b, t: (layer, 0, 0)),
    ]
    args = [qkr, qkr, v, qkr, loga, s0, out_g, s_all]
    in_specs.append(pl.BlockSpec(memory_space=pl.ANY))
    aliases = {len(args) - 1: 1}
    vmem = (2 * (3 * _nbytes((tc, GLA_HK), F32) + _nbytes((tc, GLA_HV), F32)
                 + 2 * _nbytes((tc, GLA_HV), BF16)
                 + 2 * _nbytes((GLA_HEADS, GLA_DK, GLA_DV), F32))
            + (tc // GLA_BLOCK) * (8 * _nbytes((GLA_BLOCK, GLA_HK), F32) + _nbytes((GLA_BLOCK, GLA_HV), F32)
                                   + _nbytes((GLA_HEADS, GLA_DK, GLA_DV), F32))
            + 8 * _nbytes((GLA_DK, GLA_DV), F32))
    return pl.pallas_call(
        functools.partial(_gla_rec_kernel, n_blocks=tc // GLA_BLOCK),
        grid=(n_seq, nt),
        in_specs=in_specs,
        out_specs=[pl.BlockSpec((tc, GLA_HV), row_map(0)), st_spec],
        out_shape=[jax.ShapeDtypeStruct((m, GLA_HV), BF16),
                   jax.ShapeDtypeStruct((s0.shape[0], n_seq, GLA_HEADS, GLA_DK, GLA_DV), F32)],
        input_output_aliases=aliases,
        compiler_params=_params(ARB2, vmem),
        name="gla_rec",
    )(*args)


def _out_proj_kernel(x_ref, w_ref, res_ref, g_ref, h_ref, xn_ref):
    h = res_ref[...] + _dot(x_ref[...], w_ref[...])
    h_ref[...] = h
    xn_ref[...] = _rms(h, g_ref[...]).astype(BF16)


def _out_proj(x, w, res, g_ffn, layer_w, layer_g):
    m = x.shape[0]
    tm = ROW_TM
    vmem = (2 * (2 * _nbytes((tm, D_MODEL), BF16) + 2 * _nbytes((tm, D_MODEL), F32)
                 + _nbytes((D_MODEL, D_MODEL), BF16))
            + 3 * _nbytes((tm, D_MODEL), F32))
    row = pl.BlockSpec((tm, D_MODEL), lambda i: (i, 0))
    return pl.pallas_call(
        _out_proj_kernel,
        grid=(m // tm,),
        in_specs=[
            row,
            pl.BlockSpec((None, D_MODEL, D_MODEL), lambda i: (layer_w, 0, 0)),
            row,
            pl.BlockSpec((None, 1, D_MODEL), lambda i: (layer_g, 0, 0)),
        ],
        out_specs=[row, row],
        out_shape=[jax.ShapeDtypeStruct((m, D_MODEL), F32),
                   jax.ShapeDtypeStruct((m, D_MODEL), BF16)],
        compiler_params=_params(ARB1, vmem),
        name="out_proj",
    )(x, w, res, g_ffn)


def _gm_out_kernel(z_ref, lng_ref, lnb_ref, ws_ref, bs_ref, wo_ref, res_ref, g_ref,
                   h_ref, xn_ref, *rest, chunk, n_chunks, emit_v):
    if emit_v:
        vn_ref, gate_ref = rest
    else:
        (gate_ref,) = rest
    v = z_ref[:, GM_WIDTH:].astype(F32)
    mu = jnp.mean(v, axis=-1, keepdims=True)
    vc = v - mu
    var = jnp.mean(vc * vc, axis=-1, keepdims=True)
    vn = vc * lax.rsqrt(var + EPS) * lng_ref[...] + lnb_ref[...]
    if emit_v:
        vn_ref[...] = vn
    vb = vn.astype(BF16)

    row = lax.broadcasted_iota(jnp.int32, (chunk, chunk), 0)
    col = lax.broadcasted_iota(jnp.int32, (chunk, chunk), 1)
    tril = (row >= col).astype(F32)
    for gi in range(GM_GROUPS):
        cs = slice(gi * GM_GROUP_DIM, (gi + 1) * GM_GROUP_DIM)
        wsg = (ws_ref[gi] * tril).astype(BF16)
        bsg = bs_ref[:, gi:gi + 1]
        for c in range(n_chunks):
            rows = slice(c * chunk, (c + 1) * chunk)
            mixed = _dot(wsg, vb[rows, cs]) + bsg
            gate_ref[rows, cs] = (z_ref[rows, cs].astype(F32) * mixed).astype(BF16)

    h = res_ref[...] + _dot(gate_ref[...], wo_ref[...])
    h_ref[...] = h
    xn_ref[...] = _rms(h, g_ref[...]).astype(BF16)


def _gm_out(z, ln_g, ln_b, ws, bs_t, wo, res, g_ffn, layer_m, layer_g, chunk, emit_v):
    m = z.shape[0]
    tm = WIDE_TM
    row = pl.BlockSpec((tm, D_MODEL), lambda i: (i, 0))
    vec = pl.BlockSpec((None, 1, D_MODEL), lambda i: (layer_m, 0, 0))
    vmem = (2 * (_nbytes((tm, 2 * GM_WIDTH), BF16) + 3 * _nbytes((tm, D_MODEL), F32)
                 + _nbytes((tm, D_MODEL), BF16) + _nbytes((D_MODEL, D_MODEL), BF16)
                 + _nbytes((GM_GROUPS, chunk, chunk), F32))
            + 2 * _nbytes((tm, D_MODEL), BF16) + 4 * _nbytes((tm, D_MODEL), F32))
    out_specs = [row, row]
    out_shape = [jax.ShapeDtypeStruct((m, D_MODEL), F32), jax.ShapeDtypeStruct((m, D_MODEL), BF16)]
    if emit_v:
        out_specs.append(row)
        out_shape.append(jax.ShapeDtypeStruct((m, GM_WIDTH), F32))
    return pl.pallas_call(
        functools.partial(_gm_out_kernel, chunk=chunk, n_chunks=tm // chunk, emit_v=emit_v),
        grid=(m // tm,),
        in_specs=[
            pl.BlockSpec((tm, 2 * GM_WIDTH), lambda i: (i, 0)),
            vec,
            vec,
            pl.BlockSpec((GM_GROUPS, chunk, chunk), lambda i: (0, 0, 0)),
            pl.BlockSpec((chunk, GM_GROUPS), lambda i: (0, 0)),
            pl.BlockSpec((None, D_MODEL, D_MODEL), lambda i: (layer_m, 0, 0)),
            row,
            pl.BlockSpec((None, 1, D_MODEL), lambda i: (layer_g, 0, 0)),
        ],
        out_specs=out_specs,
        out_shape=out_shape,
        scratch_shapes=[pltpu.VMEM((tm, GM_WIDTH), BF16)],
        compiler_params=_params(ARB1, vmem),
        name="gm_out",
    )(z, ln_g, ln_b, ws, bs_t, wo, res, g_ffn)


PAST_ROWS = V7X_SUBLANES
FFN_CHUNK_ROWS = 512
FFN_TF = 512


def _ffn_kernel(x_ref, wg_ref, wv_ref, cwg_ref, cwv_ref, cbg_ref, cbv_ref, wd_ref,
                pg_ref, pv_ref, f_ref, csg_ref, csv_ref,
                workg_ref, workv_ref, act_ref, carg_ref, carv_ref,
                *, n_seq, seq_rows, tiles_per_seq, chunk_rows):
    i = pl.program_id(0)
    j = pl.program_id(1)
    stride = seq_rows + PAST_ROWS
    tm = n_seq * seq_rows
    chunks = [(r0, chunk_rows) for r0 in range(0, tm, chunk_rows)]

    def pieces(r0, rows):
        step = min(rows, seq_rows)
        return [(t, step, t + PAST_ROWS * (t // seq_rows + 1)) for t in range(r0, r0 + rows, step)]

    @pl.when(j == 0)
    def _():
        f_ref[...] = jnp.zeros_like(f_ref)

    if tiles_per_seq > 1:
        @pl.when((i == 0) & (j == 0))
        def _():
            carg_ref[...] = jnp.zeros_like(carg_ref)
            carv_ref[...] = jnp.zeros_like(carv_ref)

    halves = ((wg_ref, pg_ref, carg_ref, csg_ref, workg_ref), (wv_ref, pv_ref, carv_ref, csv_ref, workv_ref))
    for w_ref, past_ref, car_ref, cs_ref, work_ref in halves:
        if tiles_per_seq > 1:
            first = (i % tiles_per_seq) == 0
            work_ref[0:PAST_ROWS, :] = jnp.where(first, past_ref[0], car_ref[j])
        else:
            for s in range(n_seq):
                work_ref[s * stride:s * stride + PAST_ROWS, :] = past_ref[s]
    for r0, rows in chunks:
        for w_ref, past_ref, car_ref, cs_ref, work_ref in halves:
            a = _dot(x_ref[r0:r0 + rows, :], w_ref[...])
            for t, n, wrow in pieces(r0, rows):
                work_ref[wrow:wrow + n, :] = a[t - r0:t - r0 + n, :]
    for w_ref, past_ref, car_ref, cs_ref, work_ref in halves:
        for s in range(n_seq):
            cs_ref[s] = work_ref[s * stride + seq_rows:(s + 1) * stride, :]
        if tiles_per_seq > 1:
            car_ref[j] = work_ref[seq_rows:stride, :]

    def conv(work_ref, cw_ref, cb_ref, wrow, n):
        x0 = work_ref[wrow:wrow + n, :]
        x1 = work_ref[wrow - 1:wrow - 1 + n, :]
        x2 = work_ref[wrow - 2:wrow - 2 + n, :]
        return cb_ref[...] + x2 * cw_ref[0:1, :] + x1 * cw_ref[1:2, :] + x0 * cw_ref[2:3, :]

    wd = wd_ref[...].astype(BF16)
    for r0, rows in chunks:
        for t, n, wrow in pieces(r0, rows):
            cg = conv(workg_ref, cwg_ref, cbg_ref, wrow, n)
            cv = conv(workv_ref, cwv_ref, cbv_ref, wrow, n)
            act_ref[t:t + n, :] = (cg * jax.nn.sigmoid(cg) * cv).astype(BF16)
        f_ref[r0:r0 + rows, :] += _dot(act_ref[r0:r0 + rows, :], wd)


def _ffn(xn, w_up, conv_w, conv_b, w_down, past, layer, seq_len):
    m = xn.shape[0]
    tm, tf = STREAM_TM, FFN_TF
    nj = D_FF // tf
    if seq_len >= tm:
        n_seq, seq_rows, tiles_per_seq = 1, tm, seq_len // tm
    else:
        n_seq, seq_rows, tiles_per_seq = tm // seq_len, seq_len, 1
    n_tiles = m // tm
    past_map = lambda off: (lambda i, j: (i // tiles_per_seq, 0, j + off))
    cs_spec = pl.BlockSpec((n_seq, PAST_ROWS, tf), lambda i, j: (i, 0, j))
    cs_shape = jax.ShapeDtypeStruct((n_tiles * n_seq, PAST_ROWS, D_FF), F32)
    work = pltpu.VMEM((n_seq * (seq_rows + PAST_ROWS), tf), F32)
    car = pltpu.VMEM((nj, PAST_ROWS, tf), F32)
    vmem = (2 * (_nbytes((tm, D_MODEL), BF16) + 2 * _nbytes((D_MODEL, tf), BF16) + _nbytes((tf, D_MODEL), F32)
                 + _nbytes((tm, D_MODEL), F32) + 4 * _nbytes((n_seq, PAST_ROWS, tf), F32))
            + _nbytes((tf, D_MODEL), BF16)
            + 2 * _nbytes((tm + n_seq * PAST_ROWS, tf), F32) + _nbytes((tm, tf), BF16)
            + 2 * _nbytes((nj, PAST_ROWS, tf), F32)
            + 4 * _nbytes((tm, tf), F32))
    f, csg, csv = pl.pallas_call(
        functools.partial(_ffn_kernel, n_seq=n_seq, seq_rows=seq_rows, tiles_per_seq=tiles_per_seq,
                          chunk_rows=FFN_CHUNK_ROWS),
        grid=(n_tiles, nj),
        in_specs=[
            pl.BlockSpec((tm, D_MODEL), lambda i, j: (i, 0)),
            pl.BlockSpec((D_MODEL, tf), lambda i, j: (0, j)),
            pl.BlockSpec((D_MODEL, tf), lambda i, j: (0, j + nj)),
            pl.BlockSpec((None, CONV_W, tf), lambda i, j: (layer, 0, j)),
            pl.BlockSpec((None, CONV_W, tf), lambda i, j: (layer, 0, j + nj)),
            pl.BlockSpec((None, 1, tf), lambda i, j: (layer, 0, j)),
            pl.BlockSpec((None, 1, tf), lambda i, j: (layer, 0, j + nj)),
            pl.BlockSpec((None, tf, D_MODEL), lambda i, j: (layer, j, 0)),
            pl.BlockSpec((n_seq, PAST_ROWS, tf), past_map(0)),
            pl.BlockSpec((n_seq, PAST_ROWS, tf), past_map(nj)),
        ],
        out_specs=[pl.BlockSpec((tm, D_MODEL), lambda i, j: (i, 0)), cs_spec, cs_spec],
        out_shape=[jax.ShapeDtypeStruct((m, D_MODEL), F32), cs_shape, cs_shape],
        scratch_shapes=[work, work, pltpu.VMEM((tm, tf), BF16), car, car],
        compiler_params=_params(ARB2, vmem),
        name="ffn",
    )(xn, w_up, w_up, conv_w, conv_w, conv_b, conv_b, w_down, past, past)
    keep = slice(tiles_per_seq - 1, None, tiles_per_seq)
    cstate = jnp.concatenate([csg[keep], csv[keep]], axis=-1)[:, PAST_ROWS - (CONV_W - 1):, :]
    return f, cstate


def _pe_kernel(h1_ref, f_ref, p_ref, g_ref, wg_ref, wp_ref, gnext_ref, *out_refs, final):
    h2 = h1_ref[...] + f_ref[...]
    hn = _rms(h2, g_ref[...]).astype(BF16)
    gate = jax.nn.sigmoid(_dot(hn, wg_ref[...]))
    proj = _dot(p_ref[...].astype(BF16), wp_ref[...])
    h3 = h2 + gate * proj
    if final:
        (y_ref,) = out_refs
        y_ref[...] = _rms(h3, gnext_ref[...])
    else:
        h_ref, xn_ref = out_refs
        h_ref[...] = h3
        xn_ref[...] = _rms(h3, gnext_ref[...]).astype(BF16)


def _pe(h1, f, p, g_pe, w_gate, w_proj, g_next, layer, layer_next):
    m = h1.shape[0]
    tm = WIDE_TM
    final = layer_next is None
    row = pl.BlockSpec((tm, D_MODEL), lambda i: (i, 0))
    vmem = (2 * (4 * _nbytes((tm, D_MODEL), F32) + _nbytes((tm, PE_DIM), F32)
                 + _nbytes((D_MODEL, D_MODEL), BF16) + _nbytes((PE_DIM, D_MODEL), BF16))
            + 6 * _nbytes((tm, D_MODEL), F32))
    if final:
        gnext_spec = pl.BlockSpec((1, D_MODEL), lambda i: (0, 0))
        out_specs = [row]
        out_shape = [jax.ShapeDtypeStruct((m, D_MODEL), F32)]
    else:
        gnext_spec = pl.BlockSpec((None, 1, D_MODEL), lambda i: (layer_next, 0, 0))
        out_specs = [row, row]
        out_shape = [jax.ShapeDtypeStruct((m, D_MODEL), F32), jax.ShapeDtypeStruct((m, D_MODEL), BF16)]
    return pl.pallas_call(
        functools.partial(_pe_kernel, final=final),
        grid=(m // tm,),
        in_specs=[
            row,
            row,
            pl.BlockSpec((None, tm, PE_DIM), lambda i: (layer, i, 0)),
            pl.BlockSpec((None, 1, D_MODEL), lambda i: (layer, 0, 0)),
            pl.BlockSpec((None, D_MODEL, D_MODEL), lambda i: (layer, 0, 0)),
            pl.BlockSpec((None, PE_DIM, D_MODEL), lambda i: (layer, 0, 0)),
            gnext_spec,
        ],
        out_specs=out_specs,
        out_shape=out_shape,
        compiler_params=_params(ARB1, vmem),
        name="pe",
    )(h1, f, p, g_pe, w_gate, w_proj, g_next)


def _trunk(x, p, gla_s0, conv_past, w, n_seq, seq_len, emit_v, w_up_bf16=None):
    m = n_seq * seq_len
    h = x.reshape(m, D_MODEL)
    p = p.reshape(DEPTH, m, PE_DIM)
    chunk = min(seq_len, GM_CHUNK)
    conv_states, gm_rows = [], []
    gla_states = jnp.zeros((gla_s0.shape[0], n_seq, GLA_HEADS, GLA_DK, GLA_DV), F32)
    make_w_up = w_up_bf16 is None
    if make_w_up:
        w_up_bf16 = []
    xn = _norm0(h, w["norm_mix"])
    for i in range(DEPTH):
        jm = i // 2
        cast = (w["ffn_w_up"], i) if make_w_up else None
        if i % 2 == 0:
            qkr = _proj(xn, w["gla_w_in_t"], jm, F32, 4, col_tile=lambda j: j + 2 * (j // 2), w_is_nk=True,
                        cast=cast, name="gla_qkr")
            if make_w_up:
                qkr, w_up_i = qkr
                w_up_bf16.append(w_up_i)
            v = _proj(xn, w["gla_w_in_t"], jm, BF16, 2, col_tile=lambda j: j + 2, w_is_nk=True, name="gla_v")
            loga = _gla_gate(xn, w["gla_w_gd"], w["gla_w_gate"], w["gla_b_gate"], jm)
            og, gla_states = _gla_rec(qkr, v, loga, gla_s0, w["gla_out_norm"], jm, n_seq, seq_len,
                                      s_all=gla_states)
            h1, xnf = _out_proj(og, w["gla_w_o"], h, w["norm_ffn"], jm, i)
        else:
            z = _proj(xn, w["gm_w_in"], jm, BF16, 2 * GM_WIDTH // PROJ_TN, bias=w["gm_b_in"],
                      act=jax.nn.gelu, cast=cast, name="gm_in")
            if make_w_up:
                z, w_up_i = z
                w_up_bf16.append(w_up_i)
            ws =w["gm_w_s"][jm][:, :chunk, :chunk]
            bs_t = w["gm_b_s"][jm][:, :chunk].T
            outs = _gm_out(z, w["gm_ln_g"], w["gm_ln_b"], ws, bs_t, w["gm_w_o"], h,
                           w["norm_ffn"], jm, i, chunk, emit_v)
            h1, xnf = outs[0], outs[1]
            if emit_v:
                gm_rows.append(outs[2].reshape(n_seq, seq_len, GM_WIDTH))
        f, cstate = _ffn(xnf, w_up_bf16[i], w["ffn_conv_w"], w["ffn_conv_b"], w["ffn_w_down"],
                         conv_past[i], i, seq_len)
        conv_states.append(cstate)
        if i + 1 < DEPTH:
            h, xn = _pe(h1, f, p, w["norm_pe"], w["pe_w_gate"], w["pe_w_proj"], w["norm_mix"], i, i + 1)
        else:
            (y,) = _pe(h1, f, p, w["norm_pe"], w["pe_w_gate"], w["pe_w_proj"], w["norm_final"], i, None)
    y = y.reshape(n_seq, seq_len, D_MODEL)
    gm_v = jnp.stack(gm_rows) if emit_v else None
    return y, gla_states, jnp.stack(conv_states), gm_v, w_up_bf16


def _pad_past(past):
    return jnp.pad(past, ((0, 0), (0, 0), (PAST_ROWS - (CONV_W - 1), 0), (0, 0)))


def kernel(x_prompt, x_sample, p_prompt, p_sample, state_gla, state_ffn_conv, norm_mix, norm_ffn, norm_pe, norm_final, gla_w_in, gla_w_gate, gla_b_gate, gla_out_norm, gla_w_o, gm_w_in, gm_b_in, gm_ln_g, gm_ln_b, gm_w_s, gm_b_s, gm_w_o, ffn_w_up, ffn_conv_w, ffn_conv_b, ffn_w_down, pe_w_proj, pe_w_gate):
    r_end = 2 * GLA_HK + 2 * GLA_HV
    w = {
        "norm_mix": norm_mix[:, None, :],
        "norm_ffn": norm_ffn[:, None, :],
        "norm_pe": norm_pe[:, None, :],
        "norm_final": norm_final[None, :],
        "gla_w_in_t": jnp.swapaxes(gla_w_in, 1, 2),
        "gla_w_gd": jnp.pad(gla_w_in[:, :, r_end:], ((0, 0), (0, 0), (0, V7X_LANES - GLA_RANK))).astype(BF16),
        "gla_w_gate": jnp.pad(gla_w_gate, ((0, 0), (0, V7X_LANES - GLA_RANK), (0, 0))).astype(BF16),
        "gla_b_gate": gla_b_gate[:, None, :],
        "gla_out_norm": gla_out_norm[:, None, :],
        "gla_w_o": gla_w_o.astype(BF16),
        "gm_w_in": gm_w_in,
        "gm_b_in": gm_b_in[:, None, :],
        "gm_ln_g": gm_ln_g[:, None, :],
        "gm_ln_b": gm_ln_b[:, None, :],
        "gm_w_s": gm_w_s,
        "gm_b_s": gm_b_s,
        "gm_w_o": gm_w_o.astype(BF16),
        "ffn_w_up": ffn_w_up,
        "ffn_conv_w": ffn_conv_w,
        "ffn_conv_b": ffn_conv_b[:, None, :],
        "ffn_w_down": ffn_w_down,
        "pe_w_proj": pe_w_proj.astype(BF16),
        "pe_w_gate": pe_w_gate.astype(BF16),
    }
    b_prompt, seq = x_prompt.shape[0], x_prompt.shape[1]
    b_sample, dec_seq = x_sample.shape[0], x_sample.shape[1]
    n_gla = state_gla.shape[0]
    gla_s0_prompt = jnp.zeros((n_gla, b_prompt, GLA_HEADS, GLA_DK, GLA_DV), F32)
    conv_s0_prompt = jnp.zeros((DEPTH, b_prompt, PAST_ROWS, 2 * D_FF), F32)
    y_prompt, gla_state_prompt, conv_state_prompt, _, w_up_bf16 = _trunk(
        x_prompt, p_prompt, gla_s0_prompt, conv_s0_prompt, w, b_prompt, seq, False)
    y_sample, gla_state_sample, conv_state_sample, gm_v_sample, _ = _trunk(
        x_sample, p_sample, state_gla, _pad_past(state_ffn_conv), w, b_sample, dec_seq, True, w_up_bf16)
    return (y_prompt, y_sample, gla_state_prompt, gla_state_sample,
            conv_state_prompt, conv_state_sample, gm_v_sample)
```
